```python
import jax, jax.numpy as jnp
from jax import lax
import numpy as np

D_MODEL = 2048
BATCH = 4
SEQ = 2048
DEPTH = 4
DEC_BATCH = 128
DEC_SEQ = 1
PAST_LEN = 8192
PAGE_SIZE = 128

CONV_DIM = D_MODEL // 2
CONV_WIDTH = 3
MLA_HEADS = 16
MLA_NOPE = 64
MLA_ROPE = 32
MLA_QK = MLA_NOPE + MLA_ROPE
MLA_V = 64
MLA_Q_RANK = D_MODEL // 4
MLA_KV_RANK = D_MODEL // 8
MLA_CACHE_DIM = MLA_KV_RANK + MLA_ROPE
MLA_QBLOCK = 128
ROPE_THETA = 10000.0
GLA_HEADS = 4
GLA_DK = D_MODEL // 16
GLA_DV = D_MODEL // 8
GLA_GATE_RANK = 16
GLA_TAU = 16.0
GLA_CHUNK = 64
D_FF = ((8 * D_MODEL // 3 + 255) // 256) * 256
PLE_DIM = 256
EPS = 1e-6

IN_SPLITS = (CONV_DIM, CONV_DIM, CONV_DIM,
             MLA_Q_RANK, MLA_CACHE_DIM,
             GLA_HEADS * GLA_DK, GLA_HEADS * GLA_DK,
             GLA_HEADS * GLA_DV, GLA_GATE_RANK,
             GLA_HEADS * GLA_DV,
             D_MODEL, D_MODEL, D_MODEL)
IN_DIM = sum(IN_SPLITS)

kernel_name = 'hybrid_conv_mla_gla_parallel_decode_step'


def rmsnorm(x, g):
    xf = x.astype(jnp.float32)
    y = xf * lax.rsqrt(jnp.mean(xf * xf, axis=-1, keepdims=True) + EPS)
    return (y * g.astype(jnp.float32)).astype(x.dtype)


def rope(x, pos):
    half = x.shape[-1] // 2
    inv = ROPE_THETA ** (-jnp.arange(half, dtype=jnp.float32) / half)
    ang = pos.astype(jnp.float32)[:, None] * inv[None, :]
    cos = jnp.cos(ang)[None, :, None, :]
    sin = jnp.sin(ang)[None, :, None, :]
    xf = x.astype(jnp.float32)
    x1, x2 = xf[..., :half], xf[..., half:]
    return jnp.concatenate([x1 * cos - x2 * sin, x1 * sin + x2 * cos], axis=-1).astype(x.dtype)


def causal_dwconv(u, w, prev):
    T = u.shape[1]
    full = jnp.concatenate([prev.astype(u.dtype), u], axis=1)
    out = full[:, 0:T] * w[0]
    for k in range(1, CONV_WIDTH):
        out = out + full[:, k:k + T] * w[k]
    return out, full[:, T:]


def gla_scan(q, k, v, log_a, s0, chunk):
    B, T, H, DK = q.shape
    DV = v.shape[-1]
    n = T // chunk

    def to_chunks(a):
        return jnp.moveaxis(a.reshape(B, n, chunk, *a.shape[2:]), 1, 0)

    causal = jnp.tril(jnp.ones((chunk, chunk), dtype=bool))

    def step(S, xs):
        qc, kc, vc, gc = xs
        qf, kf, vf = qc.astype(jnp.float32), kc.astype(jnp.float32), vc.astype(jnp.float32)
        b = jnp.cumsum(gc, axis=1)
        o_inter = jnp.einsum('bthk,bhkv->bthv', qf * jnp.exp(b), S)
        diff = b[:, :, None] - b[:, None, :]
        decay = jnp.exp(jnp.where(causal[None, :, :, None, None], diff, -jnp.inf))
        att = jnp.einsum('bthk,bshk,btshk->btsh', qf, kf, decay)
        o_intra = jnp.einsum('btsh,bshv->bthv', att, vf)
        b_last = b[:, -1]
        S_new = jnp.exp(b_last)[..., None] * S + jnp.einsum(
            'bshk,bshv->bhkv', kf * jnp.exp(b_last[:, None] - b), vf)
        return S_new, o_inter + o_intra

    S, o = lax.scan(step, s0.astype(jnp.float32),
                    (to_chunks(q), to_chunks(k), to_chunks(v), to_chunks(log_a)))
    o = jnp.moveaxis(o, 0, 1).reshape(B, T, H, DV)
    return o, S


def mla_prompt_attn(q, k, v):
    B, S, H, _ = q.shape
    nb = S // MLA_QBLOCK
    kpos = jnp.arange(S)
    scale = MLA_QK ** -0.5

    def block(i):
        qb = lax.dynamic_slice_in_dim(q, i * MLA_QBLOCK, MLA_QBLOCK, axis=1)
        qpos = i * MLA_QBLOCK + jnp.arange(MLA_QBLOCK)
        s = jnp.einsum('bqhd,bkhd->bhqk', qb, k).astype(jnp.float32) * scale
        s = jnp.where(kpos[None, :] <= qpos[:, None], s, -jnp.inf)
        pr = jax.nn.softmax(s, axis=-1).astype(v.dtype)
        return jnp.einsum('bhqk,bkhd->bqhd', pr, v)

    o = lax.map(block, jnp.arange(nb))
    return jnp.moveaxis(o, 0, 1).reshape(B, S, H * MLA_V)


def mla_decode_attn(q_nope, q_rope, ckv_new, kr_new, past, w_uk, w_uv):
    B, T = q_nope.shape[0], q_nope.shape[1]
    L = past.shape[1]
    ckv_p, kr_p = past[..., :MLA_KV_RANK], past[..., MLA_KV_RANK:]
    scale = MLA_QK ** -0.5
    q_lat = jnp.einsum('bthd,rhd->bthr', q_nope, w_uk)
    s_past = (jnp.einsum('bthr,blr->bhtl', q_lat, ckv_p.astype(q_lat.dtype))
              + jnp.einsum('bthd,bld->bhtl', q_rope, kr_p.astype(q_rope.dtype))).astype(jnp.float32) * scale
    s_new = (jnp.einsum('bthr,bsr->bhts', q_lat, ckv_new)
             + jnp.einsum('bthd,bsd->bhts', q_rope, kr_new)).astype(jnp.float32) * scale
    s_new = jnp.where(jnp.tril(jnp.ones((T, T), dtype=bool)), s_new, -jnp.inf)
    pr = jax.nn.softmax(jnp.concatenate([s_past, s_new], axis=-1), axis=-1).astype(ckv_new.dtype)
    o_lat = (jnp.einsum('bhtl,blr->bthr', pr[..., :L], ckv_p.astype(ckv_new.dtype))
             + jnp.einsum('bhts,bsr->bthr', pr[..., L:], ckv_new))
    o = jnp.einsum('bthr,rhd->bthd', o_lat, w_uv)
    return o.reshape(B, T, MLA_HEADS * MLA_V)


def _layer(h, p_i, pos, W, i, past):
    B, T, _ = h.shape
    xn = rmsnorm(h, W['g_mix'][i])
    z = xn @ W['w_in'][i]
    split_pts = np.cumsum(np.array(IN_SPLITS))[:-1].tolist()
    (cb, cc, ch, qa, kva, gq, gk, gv, gg, gr,
     ga_conv, ga_mla, ga_gla) = jnp.split(z, split_pts, axis=-1)

    prev_conv = (jnp.zeros((B, CONV_WIDTH - 1, CONV_DIM), h.dtype) if past is None else past['conv'])
    zc, conv_new = causal_dwconv(cc * ch, W['w_conv'][i], prev_conv)
    y_conv = cb * zc

    cq = rmsnorm(qa, W['g_qa'][i])
    q = (cq @ W['w_qb'][i]).reshape(B, T, MLA_HEADS, MLA_QK)
    q_nope = q[..., :MLA_NOPE]
    q_rope = rope(q[..., MLA_NOPE:], pos)
    ckv = rmsnorm(kva[..., :MLA_KV_RANK], W['g_kva'][i])
    kr = rope(kva[..., MLA_KV_RANK:][:, :, None, :], pos)[:, :, 0, :]
    mla_new = jnp.concatenate([ckv, kr], axis=-1)
    w_kvb = W['w_kvb'][i].reshape(MLA_KV_RANK, MLA_HEADS, MLA_NOPE + MLA_V)
    w_uk, w_uv = w_kvb[..., :MLA_NOPE], w_kvb[..., MLA_NOPE:]
    if past is None:
        k_nope = jnp.einsum('btr,rhd->bthd', ckv, w_uk)
        v_mla = jnp.einsum('btr,rhd->bthd', ckv, w_uv)
        k_full = jnp.concatenate(
            [k_nope, jnp.broadcast_to(kr[:, :, None, :], (B, T, MLA_HEADS, MLA_ROPE))], axis=-1)
        y_mla = mla_prompt_attn(jnp.concatenate([q_nope, q_rope], axis=-1), k_full, v_mla)
    else:
        y_mla = mla_decode_attn(q_nope, q_rope, ckv, kr, past['mla'], w_uk, w_uv)

    q_g = (gq * (GLA_DK ** -0.5)).reshape(B, T, GLA_HEADS, GLA_DK)
    k_g = gk.reshape(B, T, GLA_HEADS, GLA_DK)
    v_g = gv.reshape(B, T, GLA_HEADS, GLA_DV)
    g_pre = (gg @ W['w_gla_g2'][i] + W['b_gla_g'][i]).astype(jnp.float32)
    log_a = (jax.nn.log_sigmoid(g_pre) / GLA_TAU).reshape(B, T, GLA_HEADS, GLA_DK)
    s0 = (jnp.zeros((B, GLA_HEADS, GLA_DK, GLA_DV), jnp.float32) if past is None else past['gla'])
    chunk = GLA_CHUNK if (past is None and T % GLA_CHUNK == 0) else T
    o_g, S = gla_scan(q_g, k_g, v_g, log_a, s0, chunk)
    o_g = rmsnorm(o_g.astype(h.dtype), W['g_gla_o'][i])
    y_gla = o_g.reshape(B, T, GLA_HEADS * GLA_DV) * jax.nn.silu(gr)

    merged = (jax.nn.sigmoid(ga_conv) * (y_conv @ W['w_br_conv'][i])
              + jax.nn.sigmoid(ga_mla) * (y_mla @ W['w_br_mla'][i])
              + jax.nn.sigmoid(ga_gla) * (y_gla @ W['w_br_gla'][i]))
    h = h + merged @ W['w_o'][i]

    xn2 = rmsnorm(h, W['g_ffn'][i])
    u = xn2 @ W['w_up'][i]
    prev_ffn = (jnp.zeros((B, CONV_WIDTH - 1, 2 * D_FF), h.dtype) if past is None else past['ffn'])
    uc, ffn_new = causal_dwconv(u, W['w_ffn_conv'][i], prev_ffn)
    gate, val = jnp.split(uc, 2, axis=-1)
    h = h + (jax.nn.silu(gate) * val) @ W['w_down'][i]

    xn3 = rmsnorm(h, W['g_ple'][i])
    h = h + jax.nn.sigmoid(xn3 @ W['w_ple_gate'][i]) * (p_i @ W['w_ple'][i])
    return h, (mla_new, conv_new, S.astype(h.dtype), ffn_new)


def _trunk(x, p, pos, W, past):
    h = x
    rows = []
    for i in range(DEPTH):
        if past is None:
            past_i = None
        else:
            b = x.shape[0]
            past_i = dict(
                mla=past['cache_mla'][i][past['page_table']].reshape(b, -1, MLA_CACHE_DIM),
                conv=past['conv'][i], gla=past['gla'][i], ffn=past['ffn'][i])
        h, st = _layer(h, p[i], pos, W, i, past_i)
        rows.append(st)
    y = rmsnorm(h, W['g_final'])
    mla_rows = jnp.stack([r[0] for r in rows])
    conv_st = jnp.stack([r[1] for r in rows])
    gla_st = jnp.stack([r[2] for r in rows])
    ffn_st = jnp.stack([r[3] for r in rows])
    return y, mla_rows, conv_st, gla_st, ffn_st


def setup_inputs(seed: int = 0) -> dict:
    key = jax.random.key(seed)
    ks = iter(jax.random.split(key, 40))
    f32 = jnp.float32

    def nrm(shape, scale=1.0):
        return jax.random.normal(next(ks), shape, f32) * scale

    def lin(fan_in, fan_out):
        return nrm((DEPTH, fan_in, fan_out), fan_in ** -0.5)

    def gain(n):
        return 1.0 + nrm((DEPTH, n), 0.01)

    n_pages = PAST_LEN // PAGE_SIZE
    n_pool = (DEC_BATCH * n_pages * 5) // 4
    page_table = jax.random.permutation(next(ks), n_pool)[:DEC_BATCH * n_pages].reshape(
        DEC_BATCH, n_pages).astype(jnp.int32)
    return {
        'x_prompt': nrm((BATCH, SEQ, D_MODEL)),
        'x_sample': nrm((DEC_BATCH, DEC_SEQ, D_MODEL)),
        'cache_mla': nrm((DEPTH, n_pool, PAGE_SIZE, MLA_CACHE_DIM)),
        'state_conv': nrm((DEPTH, DEC_BATCH, CONV_WIDTH - 1, CONV_DIM)),
        'state_gla': nrm((DEPTH, DEC_BATCH, GLA_HEADS, GLA_DK, GLA_DV), 0.5),
        'state_ffn': nrm((DEPTH, DEC_BATCH, CONV_WIDTH - 1, 2 * D_FF)),
        'page_table': page_table,
        'p_prompt': nrm((DEPTH, BATCH, SEQ, PLE_DIM)),
        'p_sample': nrm((DEPTH, DEC_BATCH, DEC_SEQ, PLE_DIM)),
        'g_mix': gain(D_MODEL),
        'w_in': lin(D_MODEL, IN_DIM),
        'w_conv': nrm((DEPTH, CONV_WIDTH, CONV_DIM), CONV_WIDTH ** -0.5),
        'g_qa': gain(MLA_Q_RANK),
        'w_qb': lin(MLA_Q_RANK, MLA_HEADS * MLA_QK),
        'g_kva': gain(MLA_KV_RANK),
        'w_kvb': lin(MLA_KV_RANK, MLA_HEADS * (MLA_NOPE + MLA_V)),
        'w_gla_g2': lin(GLA_GATE_RANK, GLA_HEADS * GLA_DK),
        'b_gla_g': nrm((DEPTH, GLA_HEADS * GLA_DK), 0.01),
        'g_gla_o': gain(GLA_DV),
        'w_br_conv': lin(CONV_DIM, D_MODEL),
        'w_br_mla': lin(MLA_HEADS * MLA_V, D_MODEL),
        'w_br_gla': lin(GLA_HEADS * GLA_DV, D_MODEL),
        'w_o': lin(D_MODEL, D_MODEL),
        'g_ffn': gain(D_MODEL),
        'w_up': lin(D_MODEL, 2 * D_FF),
        'w_ffn_conv': nrm((DEPTH, CONV_WIDTH, 2 * D_FF), CONV_WIDTH ** -0.5),
        'w_down': lin(D_FF, D_MODEL),
        'g_ple': gain(D_MODEL),
        'w_ple_gate': lin(D_MODEL, D_MODEL),
        'w_ple': lin(PLE_DIM, D_MODEL),
        'g_final': 1.0 + nrm((D_MODEL,), 0.01),
    }


def reference(x_prompt, x_sample, cache_mla, state_conv, state_gla, state_ffn, page_table,
              p_prompt, p_sample, g_mix, w_in, w_conv, g_qa, w_qb, g_kva, w_kvb, w_gla_g2,
              b_gla_g, g_gla_o, w_br_conv, w_br_mla, w_br_gla, w_o, g_ffn, w_up, w_ffn_conv,
              w_down, g_ple, w_ple_gate, w_ple, g_final):
    W = dict(g_mix=g_mix, w_in=w_in, w_conv=w_conv, g_qa=g_qa, w_qb=w_qb, g_kva=g_kva,
             w_kvb=w_kvb, w_gla_g2=w_gla_g2, b_gla_g=b_gla_g, g_gla_o=g_gla_o,
             w_br_conv=w_br_conv, w_br_mla=w_br_mla, w_br_gla=w_br_gla, w_o=w_o, g_ffn=g_ffn,
             w_up=w_up, w_ffn_conv=w_ffn_conv, w_down=w_down, g_ple=g_ple,
             w_ple_gate=w_ple_gate, w_ple=w_ple, g_final=g_final)
    pos_prompt = jnp.arange(x_prompt.shape[1], dtype=jnp.int32)
    past_len = page_table.shape[1] * cache_mla.shape[2]
    pos_sample = past_len + jnp.arange(x_sample.shape[1], dtype=jnp.int32)

    y_prompt, mla_p, conv_p, gla_p, ffn_p = _trunk(x_prompt, p_prompt, pos_prompt, W, None)
    past = dict(cache_mla=cache_mla, page_table=page_table, conv=state_conv,
                gla=state_gla, ffn=state_ffn)
    y_sample, mla_s, conv_s, gla_s, ffn_s = _trunk(x_sample, p_sample, pos_sample, W, past)
    return (y_prompt, y_sample, mla_p, mla_s, conv_p, conv_s, gla_p, gla_s, ffn_p, ffn_s)
```

```python
import functools

import jax
import jax.numpy as jnp
from jax import lax
from jax.experimental import pallas as pl
from jax.experimental.pallas import tpu as pltpu

F32 = jnp.float32
BF16 = jnp.bfloat16

D_MODEL = 2048
CONV_DIM = D_MODEL // 2
MLA_HEADS = 16
MLA_NOPE = 64
MLA_ROPE = 32
MLA_QK = MLA_NOPE + MLA_ROPE
MLA_V = 64
MLA_Q_RANK = D_MODEL // 4
MLA_KV_RANK = D_MODEL // 8
MLA_CACHE_DIM = MLA_KV_RANK + MLA_ROPE
ROPE_THETA = 10000.0
GLA_HEADS = 4
GLA_DK = D_MODEL // 16
GLA_DV = D_MODEL // 8
GLA_GATE_RANK = 16
GLA_TAU = 16.0
GLA_CHUNK = 64
GLA_SUB = GLA_CHUNK // 2
D_FF = ((8 * D_MODEL // 3 + 255) // 256) * 256
PLE_DIM = 256
EPS = 1e-6

IN_SPLITS = (CONV_DIM, CONV_DIM, CONV_DIM, MLA_Q_RANK, MLA_CACHE_DIM,
             GLA_HEADS * GLA_DK, GLA_HEADS * GLA_DK, GLA_HEADS * GLA_DV, GLA_GATE_RANK,
             GLA_HEADS * GLA_DV, D_MODEL, D_MODEL, D_MODEL)

LANES = 128
HEAD_PAD = LANES
VMEM_CAP = 60 * 1024 * 1024
HALO = 16

Z_CB, Z_CC, Z_CH = 0, CONV_DIM, 2 * CONV_DIM
Z_GA = 3 * CONV_DIM
Z_GV = Z_GA + 3 * D_MODEL
Z_GR = Z_GV + GLA_HEADS * GLA_DV
Z_QA = Z_GR + GLA_HEADS * GLA_DV
Z_GQ = Z_QA + MLA_Q_RANK
Z_GK = Z_GQ + GLA_HEADS * GLA_DK
Z_KVG = Z_GK + GLA_HEADS * GLA_DK
KVG_W = 4 * LANES
Z_GG = Z_KVG + 3 * LANES
Z_DIM = Z_KVG + KVG_W


def _pick(n, pref, mult=8):
    if n <= pref:
        return n
    for t in range(pref, 0, -1):
        if n % t == 0 and t % mult == 0:
            return t
    raise ValueError(f"no tile for {n}")


def _params(block_bytes, scratch_bytes=0, sem=None):
    need = 2 * block_bytes + scratch_bytes + (8 << 20)
    kw = dict(vmem_limit_bytes=int(min(max(need, 32 << 20), VMEM_CAP)))
    if sem is not None:
        kw["dimension_semantics"] = sem
    return pltpu.CompilerParams(**kw)


def _nbytes(shape, dtype):
    n = 1
    for s in shape:
        n *= s
    return n * jnp.dtype(dtype).itemsize


def _rmsn(x, g):
    y = x * lax.rsqrt(jnp.mean(x * x, axis=-1, keepdims=True) + EPS)
    return y * g


def _dot(a, b):
    return jnp.dot(a, b, preferred_element_type=F32)


def _dot_nt(a, b):
    return lax.dot_general(a, b, (((1,), (1,)), ((), ())), preferred_element_type=F32)


def _dot_tn(a, b):
    return lax.dot_general(a, b, (((0,), (0,)), ((), ())), preferred_element_type=F32)


def _rms_mm_kernel(x_ref, g_ref, w_ref, o_ref, xn_ref):
    @pl.when(pl.program_id(1) == 0)
    def _():
        xn_ref[...] = _rmsn(x_ref[...], g_ref[...]).astype(BF16)

    o_ref[...] = _dot(xn_ref[...], w_ref[...])


def _rms_matmul(x, g, w, layer, *, tm_pref, tn_pref, name):
    M, K = x.shape
    N = w.shape[-1]
    tm = _pick(M, tm_pref)
    tn = _pick(N, tn_pref, LANES)
    blocks = _nbytes((tm, K), F32) + _nbytes((K, tn), BF16) + _nbytes((tm, tn), F32)
    return pl.pallas_call(
        _rms_mm_kernel,
        grid=(M // tm, N // tn),
        in_specs=[pl.BlockSpec((tm, K), lambda i, j: (i, 0)),
                  pl.BlockSpec((None, 1, K), lambda i, j: (layer, 0, 0)),
                  pl.BlockSpec((None, K, tn), lambda i, j: (layer, 0, j))],
        out_specs=pl.BlockSpec((tm, tn), lambda i, j: (i, j)),
        out_shape=jax.ShapeDtypeStruct((M, N), F32),
        scratch_shapes=[pltpu.VMEM((tm, K), BF16)],
        compiler_params=_params(blocks, _nbytes((tm, K), BF16), ("parallel", "arbitrary")),
        name=name,
    )(x, g, w)


def _mm_res_kernel(x_ref, w_ref, r_ref, o_ref):
    o_ref[...] = r_ref[...] + _dot(x_ref[...].astype(BF16), w_ref[...])


def _matmul_res(x, w, layer, res, *, tm_pref, tn_pref, name):
    M, K = x.shape
    N = w.shape[-1]
    tm = _pick(M, tm_pref, 16)
    tn = _pick(N, tn_pref, LANES)
    blocks = _nbytes((tm, K), x.dtype) + _nbytes((K, tn), BF16) + 2 * _nbytes((tm, tn), F32)
    return pl.pallas_call(
        _mm_res_kernel,
        grid=(M // tm, N // tn),
        in_specs=[pl.BlockSpec((tm, K), lambda i, j: (i, 0)),
                  pl.BlockSpec((None, K, tn), lambda i, j: (layer, 0, j)),
                  pl.BlockSpec((tm, tn), lambda i, j: (i, j))],
        out_specs=pl.BlockSpec((tm, tn), lambda i, j: (i, j)),
        out_shape=jax.ShapeDtypeStruct((M, N), F32),
        compiler_params=_params(blocks, 0, ("parallel", "arbitrary")),
        name=name,
    )(x, w, res)


def _shift_rows(x, k):
    rows = lax.broadcasted_iota(jnp.int32, x.shape, 0)
    return jnp.where(rows >= k, pltpu.roll(x, k, axis=0), 0.0)


def _conv_prompt_kernel(cb_ref, cc_ref, ch_ref, w_ref, y_ref, st_ref):
    T = cb_ref.shape[0]
    x = cc_ref[...] * ch_ref[...]
    zc = _shift_rows(x, 2) * w_ref[0:1, :] + _shift_rows(x, 1) * w_ref[1:2, :] + x * w_ref[2:3, :]
    y_ref[...] = (cb_ref[...] * zc).astype(y_ref.dtype)
    st_ref[...] = x[T - 2:T, :]


def _conv_prompt(z, w_conv, layer, B, T):
    tc = 2 * LANES
    nc = CONV_DIM // tc
    blocks = 3 * _nbytes((T, tc), F32) + _nbytes((T, tc), BF16)
    return pl.pallas_call(
        _conv_prompt_kernel,
        grid=(B, nc),
        in_specs=[pl.BlockSpec((T, tc), lambda b, c: (b, Z_CB // tc + c)),
                  pl.BlockSpec((T, tc), lambda b, c: (b, Z_CC // tc + c)),
                  pl.BlockSpec((T, tc), lambda b, c: (b, Z_CH // tc + c)),
                  pl.BlockSpec((None, 3, tc), lambda b, c: (layer, 0, c))],
        out_specs=[pl.BlockSpec((T, tc), lambda b, c: (b, c)),
                   pl.BlockSpec((None, 2, tc), lambda b, c: (b, 0, c))],
        out_shape=[jax.ShapeDtypeStruct((B * T, CONV_DIM), BF16),
                   jax.ShapeDtypeStruct((B, 2, CONV_DIM), F32)],
        compiler_params=_params(blocks, 0, ("parallel", "parallel")),
        name="conv_prompt",
    )(z, z, z, w_conv)


def _conv_decode_kernel(cb_ref, cc_ref, ch_ref, st_ref, w_ref, y_ref, nst_ref):
    C = cb_ref.shape[1]
    x = cc_ref[...] * ch_ref[...]
    s0 = st_ref[:, :C]
    s1 = st_ref[:, C:]
    zc = s0 * w_ref[0:1, :] + s1 * w_ref[1:2, :] + x * w_ref[2:3, :]
    y_ref[...] = (cb_ref[...] * zc).astype(y_ref.dtype)
    nst_ref[:, :C] = s1
    nst_ref[:, C:] = x


def _conv_decode(z, state2d, w_conv, layer):
    Bd = z.shape[0]
    C = CONV_DIM
    blocks = 3 * _nbytes((Bd, C), F32) + 4 * _nbytes((Bd, C), F32) + _nbytes((Bd, C), BF16)
    return pl.pallas_call(
        _conv_decode_kernel,
        grid=(1,),
        in_specs=[pl.BlockSpec((Bd, C), lambda i: (0, Z_CB // C)),
                  pl.BlockSpec((Bd, C), lambda i: (0, Z_CC // C)),
                  pl.BlockSpec((Bd, C), lambda i: (0, Z_CH // C)),
                  pl.BlockSpec((None, Bd, 2 * C), lambda i: (layer, 0, 0)),
                  pl.BlockSpec((None, 3, C), lambda i: (layer, 0, 0))],
        out_specs=[pl.BlockSpec((Bd, C), lambda i: (0, 0)),
                   pl.BlockSpec((Bd, 2 * C), lambda i: (0, 0))],
        out_shape=[jax.ShapeDtypeStruct((Bd, C), BF16),
                   jax.ShapeDtypeStruct((Bd, 2 * C), F32)],
        compiler_params=_params(blocks),
        name="conv_decode",
    )(z, z, z, state2d, w_conv)


def _mla_prep_kernel(qa_ref, kvg_ref, gqa_ref, gkv_ref, wq_ref, wk_ref, wv_ref,
                     cq_ref, sq_ref, ck_ref, sk_ref, *out_refs, with_kv):
    q_ref, rows_ref = out_refs[0], out_refs[1]
    HP = MLA_HEADS * HEAD_PAD
    cq = _rmsn(qa_ref[...], gqa_ref[...]).astype(BF16)
    qq = _dot(cq, wq_ref[...])
    c_q, s_q = cq_ref[...], sq_ref[...]
    for h in range(MLA_HEADS):
        lo = h * HEAD_PAD
        q_ref[:, lo:lo + HEAD_PAD] = (qq[:, lo:lo + HEAD_PAD] * c_q
                                      + qq[:, HP + lo:HP + lo + HEAD_PAD] * s_q).astype(q_ref.dtype)
    ckv = _rmsn(kvg_ref[:, :MLA_KV_RANK], gkv_ref[...])
    kr = kvg_ref[:, MLA_KV_RANK:MLA_KV_RANK + LANES]
    lane = lax.broadcasted_iota(jnp.int32, kr.shape, 1)
    half = MLA_ROPE // 2
    partner = jnp.where(lane < half, pltpu.roll(kr, LANES - half, axis=1), pltpu.roll(kr, half, axis=1))
    kr = kr * ck_ref[...] + partner * sk_ref[...]
    rows_ref[:, :MLA_KV_RANK] = ckv
    rows_ref[:, MLA_KV_RANK:] = kr[:, :MLA_ROPE]
    if with_kv:
        k_ref, v_ref = out_refs[2], out_refs[3]
        ckv_b = ckv.astype(BF16)
        kin = jnp.concatenate([ckv_b, kr.astype(BF16)], axis=1)
        k_ref[...] = _dot(kin, wk_ref[...]).astype(k_ref.dtype)
        v_ref[...] = _dot(ckv_b, wv_ref[...]).astype(v_ref.dtype)


def _mla_prep(z, W, layer, tabs, T, *, with_kv):
    M = z.shape[0]
    tm = _pick(min(M, T), 512, 16)
    nt = max(T // tm, 1)
    HP = MLA_HEADS * HEAD_PAD
    HV = MLA_HEADS * MLA_V
    KIN = MLA_KV_RANK + LANES
    tab_spec = pl.BlockSpec((tm, LANES), lambda i: (i % nt, 0))
    out_specs = [pl.BlockSpec((tm, HP), lambda i: (i, 0)),
                 pl.BlockSpec((tm, MLA_CACHE_DIM), lambda i: (i, 0))]
    out_shape = [jax.ShapeDtypeStruct((M, HP), BF16),
                 jax.ShapeDtypeStruct((M, MLA_CACHE_DIM), F32)]
    if with_kv:
        out_specs += [pl.BlockSpec((tm, HP), lambda i: (i, 0)),
                      pl.BlockSpec((tm, HV), lambda i: (i, 0))]
        out_shape += [jax.ShapeDtypeStruct((M, HP), BF16),
                      jax.ShapeDtypeStruct((M, HV), BF16)]
    blocks = (_nbytes((tm, MLA_Q_RANK + KVG_W), F32) + _nbytes((MLA_Q_RANK, 2 * HP), BF16)
              + _nbytes((KIN, HP), BF16) + _nbytes((MLA_KV_RANK, HV), BF16)
              + 4 * _nbytes((tm, LANES), F32) + _nbytes((tm, 2 * HP + HV), BF16)
              + _nbytes((tm, 3 * LANES), F32) + _nbytes((tm, 2 * HP), F32))
    return pl.pallas_call(
        functools.partial(_mla_prep_kernel, with_kv=with_kv),
        grid=(M // tm,),
        in_specs=[pl.BlockSpec((tm, MLA_Q_RANK), lambda i: (i, Z_QA // MLA_Q_RANK)),
                  pl.BlockSpec((tm, KVG_W), lambda i: (i, Z_KVG // KVG_W)),
                  pl.BlockSpec((None, 1, MLA_Q_RANK), lambda i: (layer, 0, 0)),
                  pl.BlockSpec((None, 1, MLA_KV_RANK), lambda i: (layer, 0, 0)),
                  pl.BlockSpec((None, MLA_Q_RANK, 2 * HP), lambda i: (layer, 0, 0)),
                  pl.BlockSpec((None, KIN, HP), lambda i: (layer, 0, 0)),
                  pl.BlockSpec((None, MLA_KV_RANK, HV), lambda i: (layer, 0, 0)),
                  tab_spec, tab_spec, tab_spec, tab_spec],
        out_specs=out_specs,
        out_shape=out_shape,
        compiler_params=_params(blocks, 0, ("parallel",)),
        name="mla_prep" if with_kv else "mla_prep_decode",
    )(z, z, W["g_qa"], W["g_kva"], W["wq2"], W["wk"], W["wv"], *tabs)


def _mla_attn_kernel(q_ref, k_ref, v_ref, o_ref, *, tq, tk):
    qi = pl.program_id(2)
    scale = MLA_QK ** -0.5
    nkb = (qi * tq + tq + tk - 1) // tk
    vlane = lax.broadcasted_iota(jnp.int32, (tk, LANES), 1)
    qpos = qi * tq + lax.broadcasted_iota(jnp.int32, (tq, tk), 0)
    kofs = lax.broadcasted_iota(jnp.int32, (tq, tk), 1)
    out = None
    for hh in range(2):
        q = q_ref[:, hh * HEAD_PAD:(hh + 1) * HEAD_PAD]

        def body(kb, carry, hh=hh, q=q):
            m, l, acc = carry
            ks = pl.multiple_of(kb * tk, tk)
            k = k_ref[pl.ds(ks, tk), hh * HEAD_PAD:(hh + 1) * HEAD_PAD]
            v = v_ref[pl.ds(ks, tk), :]
            own = (vlane < MLA_V) if hh == 0 else (vlane >= MLA_V)
            v = jnp.where(own, v, jnp.zeros_like(v))
            s = _dot_nt(q, k) * scale
            s = jnp.where(ks + kofs <= qpos, s, -jnp.inf)
            m_new = jnp.maximum(m, jnp.max(s, axis=-1, keepdims=True))
            alpha = jnp.exp(m - m_new)
            p = jnp.exp(s - m_new)
            l = alpha * l + jnp.sum(p, axis=-1, keepdims=True)
            acc = alpha * acc + _dot(p.astype(BF16), v)
            return m_new, l, acc

        m0 = jnp.full((tq, 1), -jnp.inf, F32)
        l0 = jnp.zeros((tq, 1), F32)
        a0 = jnp.zeros((tq, LANES), F32)
        m, l, acc = lax.fori_loop(0, nkb, body, (m0, l0, a0))
        part = acc / l
        out = part if out is None else out + part
    o_ref[...] = out.astype(o_ref.dtype)


def _mla_attn(q, k, v, B, T):
    tq = _pick(T, 256, 16)
    tk = _pick(T, 512, 16)
    nq = T // tq
    blocks = (_nbytes((tq, 2 * HEAD_PAD), BF16) + _nbytes((T, 2 * HEAD_PAD), BF16)
              + _nbytes((T, LANES), BF16) + _nbytes((tq, LANES), BF16))
    return pl.pallas_call(
        functools.partial(_mla_attn_kernel, tq=tq, tk=tk),
        grid=(B, MLA_HEADS // 2, nq),
        in_specs=[pl.BlockSpec((tq, 2 * HEAD_PAD), lambda b, hp, i: (b * nq + i, hp)),
                  pl.BlockSpec((T, 2 * HEAD_PAD), lambda b, hp, i: (b, hp)),
                  pl.BlockSpec((T, LANES), lambda b, hp, i: (b, hp))],
        out_specs=pl.BlockSpec((tq, LANES), lambda b, hp, i: (b * nq + i, hp)),
        out_shape=jax.ShapeDtypeStruct((B * T, MLA_HEADS * MLA_V), BF16),
        compiler_params=_params(blocks, 8 << 20, ("parallel", "parallel", "arbitrary")),
        name="mla_attn",
    )(q, k, v)


def _absorb_kernel(q_ref, w_ref, o_ref):
    o_ref[...] = _dot(q_ref[...], w_ref[...]).astype(o_ref.dtype)


def _absorb(q, wabs, layer):
    Bd = q.shape[0]
    QC = wabs.shape[-1]
    return pl.pallas_call(
        _absorb_kernel,
        grid=(MLA_HEADS,),
        in_specs=[pl.BlockSpec((Bd, HEAD_PAD), lambda h: (0, h)),
                  pl.BlockSpec((None, None, HEAD_PAD, QC), lambda h: (layer, h, 0, 0))],
        out_specs=pl.BlockSpec((Bd, QC), lambda h: (0, h)),
        out_shape=jax.ShapeDtypeStruct((Bd, MLA_HEADS * QC), BF16),
        name="mla_absorb",
    )(q, wabs)


def _page_copy(cache_ref, buf_ref, sem_ref, layer, page, slot, j, page_size):
    return pltpu.make_async_copy(cache_ref.at[layer, page],
                                 buf_ref.at[slot, pl.ds(j * page_size, page_size), :],
                                 sem_ref.at[slot])


def _dec_attn_kernel(pt_ref, q_ref, new_ref, cache_ref, o_ref, buf_ref, sem_ref,
                     *, layer, n_pages, page_size, chunk):
    b = pl.program_id(0)
    nb = pl.num_programs(0)
    scale = MLA_QK ** -0.5
    R = MLA_KV_RANK

    def fetch(bb, slot):
        for j in range(n_pages):
            _page_copy(cache_ref, buf_ref, sem_ref, layer, pt_ref[bb * n_pages + j], slot, j,
                       page_size).start()

    @pl.when(b == 0)
    def _():
        fetch(0, 0)

    @pl.when(b + 1 < nb)
    def _():
        fetch(b + 1, (b + 1) % 2)

    slot = b % 2
    for j in range(n_pages):
        _page_copy(cache_ref, buf_ref, sem_ref, layer, 0, slot, j, page_size).wait()

    q = q_ref[0]
    q_lat = q[:, :R]
    q_rope = q[:, R:R + MLA_ROPE]

    def body(c, carry):
        m, l, acc = carry
        r0 = pl.multiple_of(c * chunk, chunk)
        rows = buf_ref[slot, pl.ds(r0, chunk), :].astype(BF16)
        ckv = rows[:, :R]
        s = (_dot_nt(q_lat, ckv) + _dot_nt(q_rope, rows[:, R:])) * scale
        m_new = jnp.maximum(m, jnp.max(s, axis=-1, keepdims=True))
        alpha = jnp.exp(m - m_new)
        p = jnp.exp(s - m_new)
        l = alpha * l + jnp.sum(p, axis=-1, keepdims=True)
        acc = alpha * acc + _dot(p.astype(BF16), ckv)
        return m_new, l, acc

    new = new_ref[0].astype(BF16)
    s_new = jnp.sum(q.astype(F32)[:, :MLA_CACHE_DIM] * new.astype(F32), axis=-1, keepdims=True) * scale
    m0 = s_new
    l0 = jnp.ones_like(s_new)
    a0 = jnp.broadcast_to(new.astype(F32)[:, :R], (MLA_HEADS, R))
    n_chunks = (n_pages * page_size) // chunk
    m, l, acc = lax.fori_loop(0, n_chunks, body, (m0, l0, a0))
    o_ref[0] = acc / l


def _dec_attn(pt_flat, qcat, rows_new, cache, layer, n_pages):
    Bd = rows_new.shape[0]
    page_size = cache.shape[2]
    L = n_pages * page_size
    QC = qcat.shape[-1] // MLA_HEADS
    chunk = _pick(L, 1024, page_size)
    q3 = qcat.reshape(Bd, MLA_HEADS, QC)
    new3 = rows_new.reshape(Bd, 1, MLA_CACHE_DIM)
    buf_bytes = 2 * L * 3 * LANES * 4
    grid_spec = pltpu.PrefetchScalarGridSpec(
        num_scalar_prefetch=1,
        grid=(Bd,),
        in_specs=[pl.BlockSpec((1, MLA_HEADS, QC), lambda b, pt: (b, 0, 0)),
                  pl.BlockSpec((1, 1, MLA_CACHE_DIM), lambda b, pt: (b, 0, 0)),
                  pl.BlockSpec(memory_space=pl.ANY)],
        out_specs=pl.BlockSpec((1, MLA_HEADS, MLA_KV_RANK), lambda b, pt: (b, 0, 0)),
        scratch_shapes=[pltpu.VMEM((2, L, MLA_CACHE_DIM), F32),
                        pltpu.SemaphoreType.DMA((2,))],
    )
    return pl.pallas_call(
        functools.partial(_dec_attn_kernel, layer=layer, n_pages=n_pages, page_size=page_size,
                          chunk=chunk),
        grid_spec=grid_spec,
        out_shape=jax.ShapeDtypeStruct((Bd, MLA_HEADS, MLA_KV_RANK), F32),
        compiler_params=_params(1 << 20, buf_bytes, ("arbitrary",)),
        name="mla_decode_attn",
    )(pt_flat, q3, new3, cache)


def _dec_out_kernel(o_ref, w_ref, y_ref):
    y_ref[...] = _dot(o_ref[...].astype(BF16), w_ref[...]).astype(y_ref.dtype)


def _dec_out(o_lat2d, wuv_bd, layer):
    Bd = o_lat2d.shape[0]
    return pl.pallas_call(
        _dec_out_kernel,
        grid=(MLA_HEADS // 2,),
        in_specs=[pl.BlockSpec((Bd, 2 * MLA_KV_RANK), lambda hp: (0, hp)),
                  pl.BlockSpec((None, None, 2 * MLA_KV_RANK, 2 * MLA_V), lambda hp: (layer, hp, 0, 0))],
        out_specs=pl.BlockSpec((Bd, 2 * MLA_V), lambda hp: (0, hp)),
        out_shape=jax.ShapeDtypeStruct((Bd, MLA_HEADS * MLA_V), BF16),
        name="mla_decode_out",
    )(o_lat2d, wuv_bd)


def _log_decay(gg, wg, bg):
    g_pre = _dot(gg.astype(BF16), wg) + bg
    return (jnp.minimum(g_pre, 0.0) - jnp.log1p(jnp.exp(-jnp.abs(g_pre)))) / GLA_TAU


def _head_out(o, g_o, gr):
    return _rmsn(o, g_o) * (gr * jax.nn.sigmoid(gr))


def _gla_diag(q, k, b):
    n = q.shape[0]
    col = lax.broadcasted_iota(jnp.int32, (n, n), 1)
    row = lax.broadcasted_iota(jnp.int32, (n, n), 0)
    a = jnp.zeros((n, n), F32)
    for s in range(n):
        w = q * k[s:s + 1, :] * jnp.exp(b - b[s:s + 1, :])
        a = jnp.where(col == s, jnp.sum(w, axis=-1, keepdims=True), a)
    return jnp.where(row >= col, a, 0.0)


def _gla_prompt_kernel(gq_ref, gk_ref, gv_ref, gr_ref, gg_ref, wg_ref, bg_ref, go_ref,
                       y_ref, s_ref, la_ref, st_ref):
    T = gq_ref.shape[0]
    C, SB = GLA_CHUNK, GLA_SUB
    la_ref[...] = _log_decay(gg_ref[...], wg_ref[...], bg_ref[...])
    st_ref[...] = jnp.zeros_like(st_ref)
    tri = (lax.broadcasted_iota(jnp.int32, (C, C), 0)
           >= lax.broadcasted_iota(jnp.int32, (C, C), 1)).astype(F32)
    g_o = go_ref[...]

    def chunk(c, carry):
        r0 = pl.multiple_of(c * C, C)
        q = gq_ref[pl.ds(r0, C), :] * (GLA_DK ** -0.5)
        k = gk_ref[pl.ds(r0, C), :]
        v = gv_ref[pl.ds(r0, C), :].astype(BF16)
        b = jnp.dot(tri, la_ref[pl.ds(r0, C), :], precision=lax.Precision.HIGHEST,
                    preferred_element_type=F32)
        b_mid = b[SB - 1:SB, :]
        b_last = b[C - 1:C, :]
        st = st_ref[...]
        o = _dot_nt((q * jnp.exp(b)).astype(BF16), st.astype(BF16))
        a00 = _gla_diag(q[:SB], k[:SB], b[:SB]).astype(BF16)
        a11 = _gla_diag(q[SB:], k[SB:], b[SB:]).astype(BF16)
        a10 = _dot_nt((q[SB:] * jnp.exp(b[SB:] - b_mid)).astype(BF16),
                      (k[:SB] * jnp.exp(b_mid - b[:SB])).astype(BF16)).astype(BF16)
        o_top = _dot(a00, v[:SB])
        o_bot = _dot(a10, v[:SB]) + _dot(a11, v[SB:])
        o = o + jnp.concatenate([o_top, o_bot], axis=0)
        y_ref[pl.ds(r0, C), :] = _head_out(o, g_o, gr_ref[pl.ds(r0, C), :]).astype(y_ref.dtype)
        kd = (k * jnp.exp(b_last - b)).astype(BF16)
        st_ref[...] = st * jnp.exp(b_last) + _dot_tn(v, kd)
        return carry

    lax.fori_loop(0, T // C, chunk, 0)
    s_ref[...] = st_ref[...].T


def _gla_prompt(z, W, layer, B, T):
    DK, DV, H = GLA_DK, GLA_DV, GLA_HEADS
    blocks = (3 * _nbytes((T, DK), F32) + 2 * _nbytes((T, DV), F32) + _nbytes((T, DV), BF16)
              + _nbytes((DK, DV), F32))
    scratch = _nbytes((T, DK), F32) + _nbytes((DV, DK), F32)
    return pl.pallas_call(
        _gla_prompt_kernel,
        grid=(B, H),
        in_specs=[pl.BlockSpec((T, DK), lambda b, h: (b, Z_GQ // DK + h)),
                  pl.BlockSpec((T, DK), lambda b, h: (b, Z_GK // DK + h)),
                  pl.BlockSpec((T, DV), lambda b, h: (b, Z_GV // DV + h)),
                  pl.BlockSpec((T, DV), lambda b, h: (b, Z_GR // DV + h)),
                  pl.BlockSpec((T, LANES), lambda b, h: (b, Z_GG // LANES)),
                  pl.BlockSpec((None, LANES, DK), lambda b, h: (layer, 0, h)),
                  pl.BlockSpec((None, 1, DK), lambda b, h: (layer, 0, h)),
                  pl.BlockSpec((None, 1, DV), lambda b, h: (layer, 0, 0))],
        out_specs=[pl.BlockSpec((T, DV), lambda b, h: (b, h)),
                   pl.BlockSpec((None, None, DK, DV), lambda b, h: (b, h, 0, 0))],
        out_shape=[jax.ShapeDtypeStruct((B * T, H * DV), BF16),
                   jax.ShapeDtypeStruct((B, H, DK, DV), F32)],
        scratch_shapes=[pltpu.VMEM((T, DK), F32), pltpu.VMEM((DV, DK), F32)],
        compiler_params=_params(blocks, scratch, ("parallel", "parallel")),
        name="gla_prompt",
    )(z, z, z, z, z, W["wg2"], W["b_gla_g"], W["g_gla_o"])


def _gla_decode_kernel(gq_ref, gk_ref, gv_ref, gr_ref, gg_ref, s_ref, wg_ref, bg_ref, go_ref,
                       y_ref, ns_ref, o_scr):
    nb = gq_ref.shape[0]
    DK, DV, H = GLA_DK, GLA_DV, GLA_HEADS
    la = _log_decay(gg_ref[...], wg_ref[...], bg_ref[...])
    a = jnp.exp(la)
    q = gq_ref[...] * (GLA_DK ** -0.5)
    k = gk_ref[...]
    v = gv_ref[...]
    qa = q * a
    pieces = [x[:, h * DK:(h + 1) * DK] for x in (a, k, qa) for h in range(H)]
    pad = LANES - 3 * H * nb
    stack = jnp.concatenate(pieces + [jnp.zeros((pad, DK), F32)], axis=0)
    cols = stack.T
    qk = q * k
    for h in range(H):
        att = jnp.sum(qk[:, h * DK:(h + 1) * DK], axis=-1, keepdims=True)
        o_scr[:, h * DV:(h + 1) * DV] = att * v[:, h * DV:(h + 1) * DV]
    for i in range(nb):
        for h in range(H):
            c = h * nb + i
            a_col = cols[:, c:c + 1]
            k_col = cols[:, H * nb + c:H * nb + c + 1]
            qa_col = cols[:, 2 * H * nb + c:2 * H * nb + c + 1]
            s_old = s_ref[i, h]
            v_row = v[i:i + 1, h * DV:(h + 1) * DV]
            ns_ref[i, h] = a_col * s_old + k_col * v_row
            o_scr[i:i + 1, h * DV:(h + 1) * DV] += jnp.sum(qa_col * s_old, axis=0, keepdims=True)
    g_o = go_ref[...]
    gr = gr_ref[...]
    for h in range(H):
        sl = slice(h * DV, (h + 1) * DV)
        y_ref[:, sl] = _head_out(o_scr[:, sl], g_o, gr[:, sl]).astype(y_ref.dtype)


def _gla_decode(z, state, W, layer):
    Bd = z.shape[0]
    DK, DV, H = GLA_DK, GLA_DV, GLA_HEADS
    nb = 8
    HK, HV = H * DK, H * DV
    blocks = (2 * _nbytes((nb, HK), F32) + 3 * _nbytes((nb, HV), F32) + _nbytes((nb, LANES), F32)
              + 2 * _nbytes((nb, H, DK, DV), F32) + _nbytes((LANES, HK), BF16))
    return pl.pallas_call(
        _gla_decode_kernel,
        grid=(Bd // nb,),
        in_specs=[pl.BlockSpec((nb, HK), lambda i: (i, Z_GQ // HK)),
                  pl.BlockSpec((nb, HK), lambda i: (i, Z_GK // HK)),
                  pl.BlockSpec((nb, HV), lambda i: (i, Z_GV // HV)),
                  pl.BlockSpec((nb, HV), lambda i: (i, Z_GR // HV)),
                  pl.BlockSpec((nb, LANES), lambda i: (i, Z_GG // LANES)),
                  pl.BlockSpec((None, nb, H, DK, DV), lambda i: (layer, i, 0, 0, 0)),
                  pl.BlockSpec((None, LANES, HK), lambda i: (layer, 0, 0)),
                  pl.BlockSpec((None, 1, HK), lambda i: (layer, 0, 0)),
                  pl.BlockSpec((None, 1, DV), lambda i: (layer, 0, 0))],
        out_specs=[pl.BlockSpec((nb, HV), lambda i: (i, 0)),
                   pl.BlockSpec((nb, H, DK, DV), lambda i: (i, 0, 0, 0))],
        out_shape=[jax.ShapeDtypeStruct((Bd, HV), F32),
                   jax.ShapeDtypeStruct((Bd, H, DK, DV), F32)],
        scratch_shapes=[pltpu.VMEM((nb, HV), F32)],
        compiler_params=_params(blocks, 0, ("parallel",)),
        name="gla_decode",
    )(z, z, z, z, z, state, W["wg2"], W["b_gla_g"], W["g_gla_o"])


def _merge_kernel(yc_ref, ym_ref, yg_ref, gc_ref, gm_ref, gg_ref, wc_ref, wm_ref, wg_ref, o_ref):
    acc = jax.nn.sigmoid(gc_ref[...]) * _dot(yc_ref[...].astype(BF16), wc_ref[...])
    acc = acc + jax.nn.sigmoid(gm_ref[...]) * _dot(ym_ref[...].astype(BF16), wm_ref[...])
    acc = acc + jax.nn.sigmoid(gg_ref[...]) * _dot(yg_ref[...].astype(BF16), wg_ref[...])
    o_ref[...] = acc.astype(o_ref.dtype)


def _merge(y_conv, y_mla, y_gla, z, W, layer):
    M = z.shape[0]
    tm = _pick(M, 512, 16)
    tn = 512
    C = y_conv.shape[1]
    ga = Z_GA // tn
    gstep = D_MODEL // tn
    blocks = (3 * _nbytes((tm, C), F32) + 3 * _nbytes((tm, tn), F32) + 3 * _nbytes((C, tn), BF16)
              + _nbytes((tm, tn), BF16))
    y_spec = pl.BlockSpec((tm, C), lambda i, j: (i, 0))
    w_spec = pl.BlockSpec((None, C, tn), lambda i, j: (layer, 0, j))
    return pl.pallas_call(
        _merge_kernel,
        grid=(M // tm, D_MODEL // tn),
        in_specs=[y_spec, y_spec, y_spec,
                  pl.BlockSpec((tm, tn), lambda i, j: (i, ga + j)),
                  pl.BlockSpec((tm, tn), lambda i, j: (i, ga + gstep + j)),
                  pl.BlockSpec((tm, tn), lambda i, j: (i, ga + 2 * gstep + j)),
                  w_spec, w_spec, w_spec],
        out_specs=pl.BlockSpec((tm, tn), lambda i, j: (i, j)),
        out_shape=jax.ShapeDtypeStruct((M, D_MODEL), BF16),
        compiler_params=_params(blocks, 0, ("parallel", "arbitrary")),
        name="merge",
    )(y_conv, y_mla, y_gla, z, z, z, W["w_br_conv"], W["w_br_mla"], W["w_br_gla"])


def _ffn_up_kernel(h_ref, halo_ref, g_ref, wg_ref, wv_ref, cg_ref, cv_ref,
                   act_ref, sg_ref, sv_ref, xn_ref, *, tiles_per_seq):
    tm = h_ref.shape[0]
    i = pl.program_id(0)

    @pl.when(pl.program_id(1) == 0)
    def _():
        g = g_ref[...]
        halo = _rmsn(halo_ref[...], g)
        halo = jnp.where(i % tiles_per_seq == 0, 0.0, halo)
        xn_ref[:HALO, :] = halo.astype(BF16)
        xn_ref[HALO:, :] = _rmsn(h_ref[...], g).astype(BF16)

    xn = xn_ref[...]
    ug = _dot(xn, wg_ref[...])
    uv = _dot(xn, wv_ref[...])

    def conv(u, c_ref):
        out = pltpu.roll(u, 2, axis=0) * c_ref[0:1, :]
        out = out + pltpu.roll(u, 1, axis=0) * c_ref[1:2, :]
        out = out + u * c_ref[2:3, :]
        return out[HALO:]

    gate = conv(ug, cg_ref)
    val = conv(uv, cv_ref)
    act_ref[...] = (gate * jax.nn.sigmoid(gate) * val).astype(act_ref.dtype)
    sg_ref[...] = ug[HALO + tm - 2:, :]
    sv_ref[...] = uv[HALO + tm - 2:, :]


def _ffn_up_prompt(h, W, layer, B, T):
    M, K = h.shape
    tm = _pick(T, 512, HALO)
    tn = _pick(D_FF, 512, LANES)
    nt = T // tm
    nj = D_FF // tn
    blocks = (_nbytes((tm + HALO, K), F32) + 2 * _nbytes((K, tn), BF16) + _nbytes((tm, tn), BF16)
              + 4 * _nbytes((tm + HALO, tn), F32))
    scratch = _nbytes((tm + HALO, K), BF16)
    hb = tm // HALO
    act, sg, sv = pl.pallas_call(
        functools.partial(_ffn_up_kernel, tiles_per_seq=nt),
        grid=(M // tm, nj),
        in_specs=[pl.BlockSpec((tm, K), lambda i, j: (i, 0)),
                  pl.BlockSpec((HALO, K), lambda i, j: (jnp.maximum(i * hb - 1, 0), 0)),
                  pl.BlockSpec((None, 1, K), lambda i, j: (layer, 0, 0)),
                  pl.BlockSpec((None, K, tn), lambda i, j: (layer, 0, j)),
                  pl.BlockSpec((None, K, tn), lambda i, j: (layer, 0, nj + j)),
                  pl.BlockSpec((None, 3, tn), lambda i, j: (layer, 0, j)),
                  pl.BlockSpec((None, 3, tn), lambda i, j: (layer, 0, nj + j))],
        out_specs=[pl.BlockSpec((tm, tn), lambda i, j: (i, j)),
                   pl.BlockSpec((None, 2, tn), lambda i, j: (i, 0, j)),
                   pl.BlockSpec((None, 2, tn), lambda i, j: (i, 0, j))],
        out_shape=[jax.ShapeDtypeStruct((M, D_FF), BF16),
                   jax.ShapeDtypeStruct((M // tm, 2, D_FF), F32),
                   jax.ShapeDtypeStruct((M // tm, 2, D_FF), F32)],
        scratch_shapes=[pltpu.VMEM((tm + HALO, K), BF16)],
        compiler_params=_params(blocks, scratch, ("parallel", "arbitrary")),
        name="ffn_up",
    )(h, h, W["g_ffn"], W["w_up"], W["w_up"], W["w_ffn_conv"], W["w_ffn_conv"])
    return act, sg[nt - 1::nt], sv[nt - 1::nt]


def _ffn_conv_decode_kernel(ug_ref, uv_ref, s0g_ref, s0v_ref, s1g_ref, s1v_ref, cg_ref, cv_ref,
                            act_ref):
    gate = s0g_ref[...] * cg_ref[0:1, :] + s1g_ref[...] * cg_ref[1:2, :] + ug_ref[...] * cg_ref[2:3, :]
    val = s0v_ref[...] * cv_ref[0:1, :] + s1v_ref[...] * cv_ref[1:2, :] + uv_ref[...] * cv_ref[2:3, :]
    act_ref[...] = (gate * jax.nn.sigmoid(gate) * val).astype(act_ref.dtype)


def _ffn_conv_decode(u, state2d, w_ffn_conv, layer):
    Bd = u.shape[0]
    tn = _pick(D_FF, 1024, LANES)
    nj = D_FF // tn
    blocks = 6 * _nbytes((Bd, tn), F32) + _nbytes((Bd, tn), BF16)
    u_spec = lambda off: pl.BlockSpec((Bd, tn), lambda j: (0, off + j))
    s_spec = lambda off: pl.BlockSpec((None, Bd, tn), lambda j: (layer, 0, off + j))
    c_spec = lambda off: pl.BlockSpec((None, 3, tn), lambda j: (layer, 0, off + j))
    return pl.pallas_call(
        _ffn_conv_decode_kernel,
        grid=(nj,),
        in_specs=[u_spec(0), u_spec(nj), s_spec(0), s_spec(nj), s_spec(2 * nj), s_spec(3 * nj),
                  c_spec(0), c_spec(nj)],
        out_specs=pl.BlockSpec((Bd, tn), lambda j: (0, j)),
        out_shape=jax.ShapeDtypeStruct((Bd, D_FF), BF16),
        compiler_params=_params(blocks, 0, ("parallel",)),
        name="ffn_conv_decode",
    )(u, u, state2d, state2d, state2d, state2d, w_ffn_conv, w_ffn_conv)


def _ple_kernel(h_ref, g_ref, wg_ref, p_ref, wp_ref, o_ref, xn_ref):
    tn = o_ref.shape[1]
    j = pl.program_id(1)

    @pl.when(j == 0)
    def _():
        xn_ref[...] = _rmsn(h_ref[...], g_ref[...]).astype(BF16)

    gate = jax.nn.sigmoid(_dot(xn_ref[...], wg_ref[...]))
    emb = _dot(p_ref[...].astype(BF16), wp_ref[...])
    o_ref[...] = h_ref[:, pl.ds(pl.multiple_of(j * tn, tn), tn)] + gate * emb


def _ple(h, p, W, layer):
    M, K = h.shape
    tm = _pick(M, 1024, 16)
    tn = 512
    blocks = (_nbytes((tm, K), F32) + _nbytes((K, tn), BF16) + _nbytes((tm, PLE_DIM), F32)
              + _nbytes((PLE_DIM, tn), BF16) + _nbytes((tm, tn), F32))
    return pl.pallas_call(
        _ple_kernel,
        grid=(M // tm, K // tn),
        in_specs=[pl.BlockSpec((tm, K), lambda i, j: (i, 0)),
                  pl.BlockSpec((None, 1, K), lambda i, j: (layer, 0, 0)),
                  pl.BlockSpec((None, K, tn), lambda i, j: (layer, 0, j)),
                  pl.BlockSpec((None, tm, PLE_DIM), lambda i, j: (layer, i, 0)),
                  pl.BlockSpec((None, PLE_DIM, tn), lambda i, j: (layer, 0, j))],
        out_specs=pl.BlockSpec((tm, tn), lambda i, j: (i, j)),
        out_shape=jax.ShapeDtypeStruct((M, K), F32),
        scratch_shapes=[pltpu.VMEM((tm, K), BF16)],
        compiler_params=_params(blocks, _nbytes((tm, K), BF16), ("parallel", "arbitrary")),
        name="ple",
    )(h, W["g_ple"], W["w_ple_gate"], p, W["w_ple"])


def _final_norm_kernel(h_ref, g_ref, o_ref):
    o_ref[...] = _rmsn(h_ref[...], g_ref[...])


def _final_norm(h, g):
    M, K = h.shape
    tm = _pick(M, 512)
    return pl.pallas_call(
        _final_norm_kernel,
        grid=(M // tm,),
        in_specs=[pl.BlockSpec((tm, K), lambda i: (i, 0)),
                  pl.BlockSpec((1, K), lambda i: (0, 0))],
        out_specs=pl.BlockSpec((tm, K), lambda i: (i, 0)),
        out_shape=jax.ShapeDtypeStruct((M, K), F32),
        compiler_params=_params(2 * _nbytes((tm, K), F32), 0, ("parallel",)),
        name="final_norm",
    )(h, g)


def _prep_weights(w):
    depth = w["w_in"].shape[0]
    H, R = MLA_HEADS, MLA_KV_RANK
    offs = [0]
    for s in IN_SPLITS:
        offs.append(offs[-1] + s)
    seg = lambda n: w["w_in"][:, :, offs[n]:offs[n + 1]]
    zeros = lambda n: jnp.zeros((depth, D_MODEL, n), F32)
    w_in = jnp.concatenate(
        [seg(0), seg(1), seg(2), seg(10), seg(11), seg(12), seg(7), seg(9), seg(3), seg(5), seg(6),
         seg(4), zeros(3 * LANES - MLA_CACHE_DIM), seg(8), zeros(LANES - GLA_GATE_RANK)],
        axis=-1).astype(BF16)
    assert w_in.shape[-1] == Z_DIM

    wq = w["w_qb"].reshape(depth, MLA_Q_RANK, H, MLA_QK)
    half = MLA_ROPE // 2
    x1 = wq[..., MLA_NOPE:MLA_NOPE + half]
    x2 = wq[..., MLA_NOPE + half:]
    pad_q = jnp.zeros((depth, MLA_Q_RANK, H, HEAD_PAD - MLA_QK), F32)
    wq_p = jnp.concatenate([wq, pad_q], axis=-1)
    wq_sw = jnp.concatenate([jnp.zeros_like(wq[..., :MLA_NOPE]), -x2, x1, pad_q], axis=-1)
    wq2 = jnp.concatenate([wq_p.reshape(depth, MLA_Q_RANK, H * HEAD_PAD),
                           wq_sw.reshape(depth, MLA_Q_RANK, H * HEAD_PAD)], axis=-1).astype(BF16)

    wkv = w["w_kvb"].reshape(depth, R, H, MLA_NOPE + MLA_V)
    w_uk, w_uv = wkv[..., :MLA_NOPE], wkv[..., MLA_NOPE:]
    wk_top = jnp.concatenate([w_uk, jnp.zeros((depth, R, H, HEAD_PAD - MLA_NOPE), F32)], axis=-1)
    eye = jnp.eye(MLA_ROPE, dtype=F32)
    copy = jnp.concatenate([jnp.zeros((MLA_ROPE, MLA_NOPE), F32), eye,
                            jnp.zeros((MLA_ROPE, HEAD_PAD - MLA_QK), F32)], axis=-1)
    copy = jnp.broadcast_to(copy[:, None, :], (MLA_ROPE, H, HEAD_PAD))
    wk_bot = jnp.concatenate([copy, jnp.zeros((LANES - MLA_ROPE, H, HEAD_PAD), F32)], axis=0)
    wk = jnp.concatenate([wk_top, jnp.broadcast_to(wk_bot[None], (depth, LANES, H, HEAD_PAD))],
                         axis=1).reshape(depth, R + LANES, H * HEAD_PAD).astype(BF16)
    wv = w_uv.reshape(depth, R, H * MLA_V).astype(BF16)

    QC = R + LANES
    uk_t = jnp.transpose(w_uk, (0, 2, 3, 1))
    top = jnp.concatenate([uk_t, jnp.zeros((depth, H, MLA_NOPE, LANES), F32)], axis=-1)
    mid = jnp.concatenate([jnp.zeros((MLA_ROPE, R), F32), eye,
                           jnp.zeros((MLA_ROPE, LANES - MLA_ROPE), F32)], axis=-1)
    mid = jnp.broadcast_to(mid[None, None], (depth, H, MLA_ROPE, QC))
    bot = jnp.zeros((depth, H, HEAD_PAD - MLA_QK, QC), F32)
    wabs = jnp.concatenate([top, mid, bot], axis=2).astype(BF16)
    uv_h = jnp.transpose(w_uv, (0, 2, 1, 3)).reshape(depth, H // 2, 2, R, MLA_V)
    zed = jnp.zeros_like(uv_h[:, :, 0])
    wuv_bd = jnp.concatenate([jnp.concatenate([uv_h[:, :, 0], zed], axis=-1),
                              jnp.concatenate([zed, uv_h[:, :, 1]], axis=-1)], axis=2).astype(BF16)

    wg2 = jnp.concatenate([w["w_gla_g2"],
                           jnp.zeros((depth, LANES - GLA_GATE_RANK, GLA_HEADS * GLA_DK), F32)],
                          axis=1).astype(BF16)
    row = lambda a: a[:, None, :]
    out = dict(w_in=w_in, wq2=wq2, wk=wk, wv=wv, wabs=wabs, wuv_bd=wuv_bd, wg2=wg2,
               g_mix=row(w["g_mix"]), g_qa=row(w["g_qa"]), g_kva=row(w["g_kva"]),
               b_gla_g=row(w["b_gla_g"]), g_gla_o=row(w["g_gla_o"]), g_ffn=row(w["g_ffn"]),
               g_ple=row(w["g_ple"]), w_conv=w["w_conv"], w_ffn_conv=w["w_ffn_conv"])
    for name in ("w_br_conv", "w_br_mla", "w_br_gla", "w_o", "w_up", "w_down", "w_ple_gate", "w_ple"):
        out[name] = w[name].astype(BF16)
    return out


def _rope_tables(pos):
    half = MLA_ROPE // 2
    inv = ROPE_THETA ** (-jnp.arange(half, dtype=F32) / half)
    ang = pos.astype(F32)[:, None] * inv[None, :]
    cos, sin = jnp.cos(ang), jnp.sin(ang)
    n = pos.shape[0]
    ones = jnp.ones((n, MLA_NOPE), F32)
    zq = jnp.zeros((n, HEAD_PAD - MLA_QK), F32)
    zk = jnp.zeros((n, LANES - MLA_ROPE), F32)
    c_q = jnp.concatenate([ones, cos, cos, zq], axis=1)
    s_q = jnp.concatenate([jnp.zeros_like(ones), sin, sin, zq], axis=1)
    c_k = jnp.concatenate([cos, cos, zk], axis=1)
    s_k = jnp.concatenate([-sin, sin, zk], axis=1)
    return c_q, s_q, c_k, s_k


def _layer_tail(h, z, y_conv, y_mla, y_gla, W, layer):
    merged = _merge(y_conv, y_mla, y_gla, z, W, layer)
    return _matmul_res(merged, W["w_o"], layer, h, tm_pref=1024, tn_pref=1024, name="out_proj")


def _prompt_trunk(x, p, W, g_final):
    B, T, D = x.shape
    depth = W["w_in"].shape[0]
    M = B * T
    h = x.reshape(M, D)
    p2 = p.reshape(depth, M, PLE_DIM)
    tabs = _rope_tables(jnp.arange(T, dtype=jnp.int32))
    rows, conv_st, gla_st, ffn_st = [], [], [], []
    for i in range(depth):
        z = _rms_matmul(h, W["g_mix"], W["w_in"], i, tm_pref=1024, tn_pref=512, name="in_proj")
        y_conv, cst = _conv_prompt(z, W["w_conv"], i, B, T)
        q, mla_rows, k, v = _mla_prep(z, W, i, tabs, T, with_kv=True)
        y_mla = _mla_attn(q, k, v, B, T)
        y_gla, s_new = _gla_prompt(z, W, i, B, T)
        h = _layer_tail(h, z, y_conv, y_mla, y_gla, W, i)
        act, sg, sv = _ffn_up_prompt(h, W, i, B, T)
        h = _matmul_res(act, W["w_down"], i, h, tm_pref=512, tn_pref=1024, name="ffn_down")
        h = _ple(h, p2, W, i)
        rows.append(mla_rows.reshape(B, T, MLA_CACHE_DIM))
        conv_st.append(cst)
        gla_st.append(s_new)
        ffn_st.append(jnp.concatenate([sg, sv], axis=-1))
    y = _final_norm(h, g_final).reshape(B, T, D)
    return y, jnp.stack(rows), jnp.stack(conv_st), jnp.stack(gla_st), jnp.stack(ffn_st)


def _decode_trunk(x, p, W, g_final, cache, page_table, st_conv, st_gla, st_ffn):
    Bd, T, D = x.shape
    assert T == 1, "decode path handles one new token per request"
    depth = W["w_in"].shape[0]
    n_pages = page_table.shape[1]
    past_len = n_pages * cache.shape[2]
    h = x.reshape(Bd, D)
    p2 = p.reshape(depth, Bd, PLE_DIM)
    tabs = _rope_tables(jnp.full((Bd,), past_len, dtype=jnp.int32))
    pt_flat = page_table.reshape(-1)
    st_conv2 = st_conv.reshape(depth, Bd, 2 * CONV_DIM)
    st_ffn2 = st_ffn.reshape(depth, Bd, 4 * D_FF)
    rows, conv_st, gla_st, ffn_st = [], [], [], []
    for i in range(depth):
        z = _rms_matmul(h, W["g_mix"], W["w_in"], i, tm_pref=1024, tn_pref=1024, name="in_proj_decode")
        y_conv, cst = _conv_decode(z, st_conv2, W["w_conv"], i)
        q, mla_rows = _mla_prep(z, W, i, tabs, Bd, with_kv=False)
        qcat = _absorb(q, W["wabs"], i)
        o_lat = _dec_attn(pt_flat, qcat, mla_rows, cache, i, n_pages)
        y_mla = _dec_out(o_lat.reshape(Bd, MLA_HEADS * MLA_KV_RANK), W["wuv_bd"], i)
        y_gla, s_new = _gla_decode(z, st_gla, W, i)
        h = _layer_tail(h, z, y_conv, y_mla, y_gla, W, i)
        u = _rms_matmul(h, W["g_ffn"], W["w_up"], i, tm_pref=1024, tn_pref=1024, name="ffn_up_decode")
        act = _ffn_conv_decode(u, st_ffn2, W["w_ffn_conv"], i)
        h = _matmul_res(act, W["w_down"], i, h, tm_pref=512, tn_pref=1024, name="ffn_down_decode")
        h = _ple(h, p2, W, i)
        rows.append(mla_rows.reshape(Bd, 1, MLA_CACHE_DIM))
        conv_st.append(cst.reshape(Bd, 2, CONV_DIM))
        gla_st.append(s_new)
        ffn_st.append(jnp.stack([st_ffn[i, :, 1, :], u], axis=1))
    y = _final_norm(h, g_final).reshape(Bd, 1, D)
    return y, jnp.stack(rows), jnp.stack(conv_st), jnp.stack(gla_st), jnp.stack(ffn_st)


def kernel(x_prompt, x_sample, cache_mla, state_conv, state_gla, state_ffn, page_table, p_prompt, p_sample, g_mix, w_in, w_conv, g_qa, w_qb, g_kva, w_kvb, w_gla_g2, b_gla_g, g_gla_o, w_br_conv, w_br_mla, w_br_gla, w_o, g_ffn, w_up, w_ffn_conv, w_down, g_ple, w_ple_gate, w_ple, g_final):
    W = _prep_weights(dict(
        g_mix=g_mix, w_in=w_in, w_conv=w_conv, g_qa=g_qa, w_qb=w_qb, g_kva=g_kva, w_kvb=w_kvb,
        w_gla_g2=w_gla_g2, b_gla_g=b_gla_g, g_gla_o=g_gla_o, w_br_conv=w_br_conv,
        w_br_mla=w_br_mla, w_br_gla=w_br_gla, w_o=w_o, g_ffn=g_ffn, w_up=w_up,
        w_ffn_conv=w_ffn_conv, w_down=w_down, g_ple=g_ple, w_ple_gate=w_ple_gate, w_ple=w_ple))
    g_fin = g_final[None, :]
    y_p, mla_p, conv_p, gla_p, ffn_p = _prompt_trunk(x_prompt, p_prompt, W, g_fin)
    y_s, mla_s, conv_s, gla_s, ffn_s = _decode_trunk(
        x_sample, p_sample, W, g_fin, cache_mla, page_table, state_conv, state_gla, state_ffn)
    return (y_p, y_s, mla_p, mla_s, conv_p, conv_s, gla_p, gla_s, ffn_p, ffn_s)
```

```python
import functools

import jax
import jax.numpy as jnp
from jax import lax
from jax.experimental import pallas as pl
from jax.experimental.pallas import tpu as pltpu

F32 = jnp.float32
BF16 = jnp.bfloat16

D_MODEL = 2048
CONV_DIM = D_MODEL // 2
MLA_HEADS = 16
MLA_NOPE = 64
MLA_ROPE = 32
MLA_QK = MLA_NOPE + MLA_ROPE
MLA_V = 64
MLA_Q_RANK = D_MODEL // 4
MLA_KV_RANK = D_MODEL // 8
MLA_CACHE_DIM = MLA_KV_RANK + MLA_ROPE
ROPE_THETA = 10000.0
GLA_HEADS = 4
GLA_DK = D_MODEL // 16
GLA_DV = D_MODEL // 8
GLA_GATE_RANK = 16
GLA_TAU = 16.0
GLA_CHUNK = 64
GLA_LEVELS = (32, 16, 8)
GLA_DIAG = 8
D_FF = ((8 * D_MODEL // 3 + 255) // 256) * 256
PLE_DIM = 256
EPS = 1e-6

IN_SPLITS = (CONV_DIM, CONV_DIM, CONV_DIM, MLA_Q_RANK, MLA_CACHE_DIM,
             GLA_HEADS * GLA_DK, GLA_HEADS * GLA_DK, GLA_HEADS * GLA_DV, GLA_GATE_RANK,
             GLA_HEADS * GLA_DV, D_MODEL, D_MODEL, D_MODEL)

LANES = 128
HEAD_PAD = LANES
VMEM_CAP = 60 * 1024 * 1024
HALO = 16
MXU_COLS = 256

Z_CB, Z_CC, Z_CH = 0, CONV_DIM, 2 * CONV_DIM
Z_GA = 3 * CONV_DIM
Z_GV = Z_GA + 3 * D_MODEL
Z_GR = Z_GV + GLA_HEADS * GLA_DV
Z_QA = Z_GR + GLA_HEADS * GLA_DV
Z_GQ = Z_QA + MLA_Q_RANK
Z_GK = Z_GQ + GLA_HEADS * GLA_DK
Z_KVG = Z_GK + GLA_HEADS * GLA_DK
KVG_W = 4 * LANES
Z_GG = Z_KVG + 3 * LANES
Z_DIM = Z_KVG + KVG_W


def _pick(n, pref, mult=8):
    if n <= pref:
        return n
    for t in range(pref, 0, -1):
        if n % t == 0 and t % mult == 0:
            return t
    raise ValueError(f"no tile for {n}")


def _params(block_bytes, scratch_bytes=0, sem=None):
    need = 2 * block_bytes + scratch_bytes + (8 << 20)
    kw = dict(vmem_limit_bytes=int(min(max(need, 32 << 20), VMEM_CAP)))
    if sem is not None:
        kw["dimension_semantics"] = sem
    return pltpu.CompilerParams(**kw)


def _nbytes(shape, dtype):
    n = 1
    for s in shape:
        n *= s
    return n * jnp.dtype(dtype).itemsize


def _rmsn(x, g):
    y = x * lax.rsqrt(jnp.mean(x * x, axis=-1, keepdims=True) + EPS)
    return y * g


def _dot(a, b):
    return jnp.dot(a, b, preferred_element_type=F32)


def _dot_nt(a, b):
    return lax.dot_general(a, b, (((1,), (1,)), ((), ())), preferred_element_type=F32)


def _dot_tn(a, b):
    return lax.dot_general(a, b, (((0,), (0,)), ((), ())), preferred_element_type=F32)


def _rms_mm_kernel(x_ref, g_ref, w_ref, o_ref, xn_ref):
    @pl.when(pl.program_id(1) == 0)
    def _():
        xn_ref[...] = _rmsn(x_ref[...], g_ref[...]).astype(BF16)

    o_ref[...] = _dot(xn_ref[...], w_ref[...])


def _rms_matmul(x, g, w, layer, *, tm_pref, tn_pref, name):
    M, K = x.shape
    N = w.shape[-1]
    tm = _pick(M, tm_pref)
    tn = _pick(N, tn_pref, LANES)
    blocks = _nbytes((tm, K), F32) + _nbytes((K, tn), BF16) + _nbytes((tm, tn), F32)
    return pl.pallas_call(
        _rms_mm_kernel,
        grid=(M // tm, N // tn),
        in_specs=[pl.BlockSpec((tm, K), lambda i, j: (i, 0)),
                  pl.BlockSpec((None, 1, K), lambda i, j: (layer, 0, 0)),
                  pl.BlockSpec((None, K, tn), lambda i, j: (layer, 0, j))],
        out_specs=pl.BlockSpec((tm, tn), lambda i, j: (i, j)),
        out_shape=jax.ShapeDtypeStruct((M, N), F32),
        scratch_shapes=[pltpu.VMEM((tm, K), BF16)],
        compiler_params=_params(blocks, _nbytes((tm, K), BF16), ("parallel", "arbitrary")),
        name=name,
    )(x, g, w)


def _mm_res_kernel(x_ref, w_ref, r_ref, o_ref):
    o_ref[...] = r_ref[...] + _dot(x_ref[...].astype(BF16), w_ref[...])


def _matmul_res(x, w, layer, res, *, tm_pref, tn_pref, name):
    M, K = x.shape
    N = w.shape[-1]
    tm = _pick(M, tm_pref, 16)
    tn = _pick(N, tn_pref, LANES)
    blocks = _nbytes((tm, K), x.dtype) + _nbytes((K, tn), BF16) + 2 * _nbytes((tm, tn), F32)
    return pl.pallas_call(
        _mm_res_kernel,
        grid=(N // tn, M // tm),
        in_specs=[pl.BlockSpec((tm, K), lambda j, i: (i, 0)),
                  pl.BlockSpec((None, K, tn), lambda j, i: (layer, 0, j)),
                  pl.BlockSpec((tm, tn), lambda j, i: (i, j))],
        out_specs=pl.BlockSpec((tm, tn), lambda j, i: (i, j)),
        out_shape=jax.ShapeDtypeStruct((M, N), F32),
        compiler_params=_params(blocks, 0, ("parallel", "parallel")),
        name=name,
    )(x, w, res)


def _shift_rows(x, k):
    rows = lax.broadcasted_iota(jnp.int32, x.shape, 0)
    return jnp.where(rows >= k, pltpu.roll(x, k, axis=0), 0.0)


def _conv_prompt_kernel(cb_ref, cc_ref, ch_ref, w_ref, y_ref, st_ref):
    T = cb_ref.shape[0]
    x = cc_ref[...] * ch_ref[...]
    zc = _shift_rows(x, 2) * w_ref[0:1, :] + _shift_rows(x, 1) * w_ref[1:2, :] + x * w_ref[2:3, :]
    y_ref[...] = (cb_ref[...] * zc).astype(y_ref.dtype)
    st_ref[...] = x[T - 2:T, :]


def _conv_prompt(z, w_conv, layer, B, T):
    tc = 2 * LANES
    nc = CONV_DIM // tc
    blocks = 3 * _nbytes((T, tc), F32) + _nbytes((T, tc), BF16)
    return pl.pallas_call(
        _conv_prompt_kernel,
        grid=(B, nc),
        in_specs=[pl.BlockSpec((T, tc), lambda b, c: (b, Z_CB // tc + c)),
                  pl.BlockSpec((T, tc), lambda b, c: (b, Z_CC // tc + c)),
                  pl.BlockSpec((T, tc), lambda b, c: (b, Z_CH // tc + c)),
                  pl.BlockSpec((None, 3, tc), lambda b, c: (layer, 0, c))],
        out_specs=[pl.BlockSpec((T, tc), lambda b, c: (b, c)),
                   pl.BlockSpec((None, 2, tc), lambda b, c: (b, 0, c))],
        out_shape=[jax.ShapeDtypeStruct((B * T, CONV_DIM), BF16),
                   jax.ShapeDtypeStruct((B, 2, CONV_DIM), F32)],
        compiler_params=_params(blocks, 0, ("parallel", "parallel")),
        name="conv_prompt",
    )(z, z, z, w_conv)


def _conv_decode_kernel(cb_ref, cc_ref, ch_ref, st_ref, w_ref, y_ref, nst_ref):
    C = cb_ref.shape[1]
    x = cc_ref[...] * ch_ref[...]
    s0 = st_ref[:, :C]
    s1 = st_ref[:, C:]
    zc = s0 * w_ref[0:1, :] + s1 * w_ref[1:2, :] + x * w_ref[2:3, :]
    y_ref[...] = (cb_ref[...] * zc).astype(y_ref.dtype)
    nst_ref[:, :C] = s1
    nst_ref[:, C:] = x


def _conv_decode(z, state2d, w_conv, layer):
    Bd = z.shape[0]
    C = CONV_DIM
    blocks = 3 * _nbytes((Bd, C), F32) + 4 * _nbytes((Bd, C), F32) + _nbytes((Bd, C), BF16)
    return pl.pallas_call(
        _conv_decode_kernel,
        grid=(1,),
        in_specs=[pl.BlockSpec((Bd, C), lambda i: (0, Z_CB // C)),
                  pl.BlockSpec((Bd, C), lambda i: (0, Z_CC // C)),
                  pl.BlockSpec((Bd, C), lambda i: (0, Z_CH // C)),
                  pl.BlockSpec((None, Bd, 2 * C), lambda i: (layer, 0, 0)),
                  pl.BlockSpec((None, 3, C), lambda i: (layer, 0, 0))],
        out_specs=[pl.BlockSpec((Bd, C), lambda i: (0, 0)),
                   pl.BlockSpec((Bd, 2 * C), lambda i: (0, 0))],
        out_shape=[jax.ShapeDtypeStruct((Bd, C), BF16),
                   jax.ShapeDtypeStruct((Bd, 2 * C), F32)],
        compiler_params=_params(blocks),
        name="conv_decode",
    )(z, z, z, state2d, w_conv)


def _mla_prep_kernel(qa_ref, kvg_ref, gqa_ref, gkv_ref, wq_ref, wk_ref, wv_ref,
                     cq_ref, sq_ref, ck_ref, sk_ref, *out_refs, with_kv):
    q_ref, rows_ref = out_refs[0], out_refs[1]
    HP = MLA_HEADS * HEAD_PAD
    cq = _rmsn(qa_ref[...], gqa_ref[...]).astype(BF16)
    qq = _dot(cq, wq_ref[...])
    c_q, s_q = cq_ref[...], sq_ref[...]
    for h in range(MLA_HEADS):
        lo = h * HEAD_PAD
        q_ref[:, lo:lo + HEAD_PAD] = (qq[:, lo:lo + HEAD_PAD] * c_q
                                      + qq[:, HP + lo:HP + lo + HEAD_PAD] * s_q).astype(q_ref.dtype)
    ckv = _rmsn(kvg_ref[:, :MLA_KV_RANK], gkv_ref[...])
    kr = kvg_ref[:, MLA_KV_RANK:MLA_KV_RANK + LANES]
    lane = lax.broadcasted_iota(jnp.int32, kr.shape, 1)
    half = MLA_ROPE // 2
    partner = jnp.where(lane < half, pltpu.roll(kr, LANES - half, axis=1), pltpu.roll(kr, half, axis=1))
    kr = kr * ck_ref[...] + partner * sk_ref[...]
    rows_ref[:, :MLA_KV_RANK] = ckv
    rows_ref[:, MLA_KV_RANK:] = kr[:, :MLA_ROPE]
    if with_kv:
        k_ref, v_ref = out_refs[2], out_refs[3]
        ckv_b = ckv.astype(BF16)
        kin = jnp.concatenate([ckv_b, kr.astype(BF16)], axis=1)
        k_ref[...] = _dot(kin, wk_ref[...]).astype(k_ref.dtype)
        v_ref[...] = _dot(ckv_b, wv_ref[...]).astype(v_ref.dtype)


def _mla_prep(z, W, layer, tabs, T, *, with_kv):
    M = z.shape[0]
    tm = _pick(min(M, T), 512, 16)
    nt = max(T // tm, 1)
    HP = MLA_HEADS * HEAD_PAD
    HV = MLA_HEADS * MLA_V
    KIN = MLA_KV_RANK + LANES
    tab_spec = pl.BlockSpec((tm, LANES), lambda i: (i % nt, 0))
    out_specs = [pl.BlockSpec((tm, HP), lambda i: (i, 0)),
                 pl.BlockSpec((tm, MLA_CACHE_DIM), lambda i: (i, 0))]
    out_shape = [jax.ShapeDtypeStruct((M, HP), BF16),
                 jax.ShapeDtypeStruct((M, MLA_CACHE_DIM), F32)]
    if with_kv:
        out_specs += [pl.BlockSpec((tm, HP), lambda i: (i, 0)),
                      pl.BlockSpec((tm, HV), lambda i: (i, 0))]
        out_shape += [jax.ShapeDtypeStruct((M, HP), BF16),
                      jax.ShapeDtypeStruct((M, HV), BF16)]
    blocks = (_nbytes((tm, MLA_Q_RANK + KVG_W), F32) + _nbytes((MLA_Q_RANK, 2 * HP), BF16)
              + _nbytes((KIN, HP), BF16) + _nbytes((MLA_KV_RANK, HV), BF16)
              + 4 * _nbytes((tm, LANES), F32) + _nbytes((tm, 2 * HP + HV), BF16)
              + _nbytes((tm, 3 * LANES), F32) + _nbytes((tm, 2 * HP), F32))
    return pl.pallas_call(
        functools.partial(_mla_prep_kernel, with_kv=with_kv),
        grid=(M // tm,),
        in_specs=[pl.BlockSpec((tm, MLA_Q_RANK), lambda i: (i, Z_QA // MLA_Q_RANK)),
                  pl.BlockSpec((tm, KVG_W), lambda i: (i, Z_KVG // KVG_W)),
                  pl.BlockSpec((None, 1, MLA_Q_RANK), lambda i: (layer, 0, 0)),
                  pl.BlockSpec((None, 1, MLA_KV_RANK), lambda i: (layer, 0, 0)),
                  pl.BlockSpec((None, MLA_Q_RANK, 2 * HP), lambda i: (layer, 0, 0)),
                  pl.BlockSpec((None, KIN, HP), lambda i: (layer, 0, 0)),
                  pl.BlockSpec((None, MLA_KV_RANK, HV), lambda i: (layer, 0, 0)),
                  tab_spec, tab_spec, tab_spec, tab_spec],
        out_specs=out_specs,
        out_shape=out_shape,
        compiler_params=_params(blocks, 0, ("parallel",)),
        name="mla_prep" if with_kv else "mla_prep_decode",
    )(z, z, W["g_qa"], W["g_kva"], W["wq2"], W["wk"], W["wv"], *tabs)


def _mla_attn_kernel(q_ref, k_ref, v_ref, o_ref, *, tq, tk):
    qi = pl.program_id(2)
    c = (MLA_QK ** -0.5) * 1.4426950408889634
    v_lo = lax.broadcasted_iota(jnp.int32, (tk, LANES), 1) < MLA_V
    o_lo = lax.broadcasted_iota(jnp.int32, (tq, LANES), 1) < MLA_V
    q0 = q_ref[:, :HEAD_PAD]
    q1 = q_ref[:, HEAD_PAD:]

    def head(q, k, m, l, valid):
        s = _dot_nt(q, k)
        if valid is not None:
            s = jnp.where(valid, s, -jnp.inf)
        m_new = jnp.maximum(m, jnp.max(s, axis=-1, keepdims=True))
        alpha = jnp.exp2((m - m_new) * c)
        p = jnp.exp2((s - m_new) * c)
        l = alpha * l + jnp.sum(p, axis=-1, keepdims=True)
        return m_new, l, alpha, p.astype(BF16)

    def step(kb, carry, masked):
        m0, l0, m1, l1, acc = carry
        ks = pl.multiple_of(kb * tk, tk)
        v = v_ref[pl.ds(ks, tk), :]
        zero = jnp.zeros_like(v)
        valid = None
        if masked:
            qpos = qi * tq + lax.broadcasted_iota(jnp.int32, (tq, tk), 0)
            valid = ks + lax.broadcasted_iota(jnp.int32, (tq, tk), 1) <= qpos
        m0, l0, a0, p0 = head(q0, k_ref[pl.ds(ks, tk), :HEAD_PAD], m0, l0, valid)
        m1, l1, a1, p1 = head(q1, k_ref[pl.ds(ks, tk), HEAD_PAD:], m1, l1, valid)
        acc = (jnp.where(o_lo, a0, a1) * acc + _dot(p0, jnp.where(v_lo, v, zero))
               + _dot(p1, jnp.where(v_lo, zero, v)))
        return m0, l0, m1, l1, acc

    neg = jnp.full((tq, 1), -jnp.inf, F32)
    nil = jnp.zeros((tq, 1), F32)
    carry = (neg, nil, neg, nil, jnp.zeros((tq, LANES), F32))
    n_full = (qi * tq) // tk
    carry = lax.fori_loop(0, n_full, functools.partial(step, masked=False), carry)
    for d in range(tq // tk):
        carry = step(n_full + d, carry, True)
    m0, l0, m1, l1, acc = carry
    o_ref[...] = (acc / jnp.where(o_lo, l0, l1)).astype(o_ref.dtype)


def _mla_attn(q, k, v, B, T):
    tq = _pick(T, 512, 16)
    tk = _pick(T, 256, 16)
    assert tq % tk == 0
    nq = T // tq
    blocks = (_nbytes((tq, 2 * HEAD_PAD), BF16) + _nbytes((T, 2 * HEAD_PAD), BF16)
              + _nbytes((T, LANES), BF16) + _nbytes((tq, LANES), BF16))
    return pl.pallas_call(
        functools.partial(_mla_attn_kernel, tq=tq, tk=tk),
        grid=(B, MLA_HEADS // 2, nq),
        in_specs=[pl.BlockSpec((tq, 2 * HEAD_PAD), lambda b, hp, i: (b * nq + i, hp)),
                  pl.BlockSpec((T, 2 * HEAD_PAD), lambda b, hp, i: (b, hp)),
                  pl.BlockSpec((T, LANES), lambda b, hp, i: (b, hp))],
        out_specs=pl.BlockSpec((tq, LANES), lambda b, hp, i: (b * nq + i, hp)),
        out_shape=jax.ShapeDtypeStruct((B * T, MLA_HEADS * MLA_V), BF16),
        compiler_params=_params(blocks, 8 << 20, ("parallel", "parallel", "arbitrary")),
        name="mla_attn",
    )(q, k, v)


def _absorb_kernel(q_ref, w_ref, o_ref):
    o_ref[...] = _dot(q_ref[...], w_ref[...]).astype(o_ref.dtype)


def _absorb(q, wabs, layer):
    Bd = q.shape[0]
    QC = wabs.shape[-1]
    return pl.pallas_call(
        _absorb_kernel,
        grid=(MLA_HEADS,),
        in_specs=[pl.BlockSpec((Bd, HEAD_PAD), lambda h: (0, h)),
                  pl.BlockSpec((None, None, HEAD_PAD, QC), lambda h: (layer, h, 0, 0))],
        out_specs=pl.BlockSpec((Bd, QC), lambda h: (0, h)),
        out_shape=jax.ShapeDtypeStruct((Bd, MLA_HEADS * QC), BF16),
        name="mla_absorb",
    )(q, wabs)


def _page_copy(cache_ref, buf_ref, sem_ref, layer, page, slot, j, page_size):
    return pltpu.make_async_copy(cache_ref.at[layer, page],
                                 buf_ref.at[slot, :, pl.ds(j * page_size, page_size)],
                                 sem_ref.at[slot])


def _dec_attn_kernel(pt_ref, q_ref, new_ref, cache_ref, o_ref, buf_ref, kb_ref, sem_ref,
                     *, layer, n_pages, page_size, chunk):
    b = pl.program_id(0)
    nb = pl.num_programs(0)
    scale = MLA_QK ** -0.5
    R = MLA_KV_RANK

    def fetch(bb, slot):
        for j in range(n_pages):
            _page_copy(cache_ref, buf_ref, sem_ref, layer, pt_ref[bb * n_pages + j], slot, j,
                       page_size).start()

    @pl.when(b == 0)
    def _():
        fetch(0, 0)

    @pl.when(b + 1 < nb)
    def _():
        fetch(b + 1, (b + 1) % 2)

    slot = b % 2
    for j in range(n_pages):
        _page_copy(cache_ref, buf_ref, sem_ref, layer, 0, slot, j, page_size).wait()

    q = q_ref[0]
    q_lat = q[:, :R]
    q_rope = q[:, R:R + MLA_ROPE]
    n_chunks = (n_pages * page_size) // chunk
    parts = []
    for c in range(n_chunks):
        cols = slice(c * chunk, (c + 1) * chunk)
        rows = buf_ref[slot, :, cols].astype(BF16)
        kb_ref[:, cols] = rows[:R]
        parts.append(_dot(q_lat, rows[:R]) + _dot(q_rope, rows[R:]))
    s = jnp.concatenate(parts, axis=1)
    new = new_ref[0].astype(BF16).astype(F32)
    s_new = jnp.sum(q.astype(F32)[:, :MLA_CACHE_DIM] * new, axis=-1, keepdims=True)
    m = jnp.maximum(jnp.max(s, axis=-1, keepdims=True), s_new)
    p = jnp.exp((s - m) * scale)
    p_new = jnp.exp((s_new - m) * scale)
    l = jnp.sum(p, axis=-1, keepdims=True) + p_new
    acc = p_new * new[:, :R]
    pb = p.astype(BF16)
    for c in range(n_chunks):
        cols = slice(c * chunk, (c + 1) * chunk)
        acc = acc + _dot_nt(pb[:, cols], kb_ref[:, cols])
    o_ref[0] = acc / l


def _dec_attn(pt_flat, qcat, rows_new, cache_t, layer, n_pages):
    Bd = rows_new.shape[0]
    page_size = cache_t.shape[3]
    L = n_pages * page_size
    QC = qcat.shape[-1] // MLA_HEADS
    chunk = _pick(L, 1024, page_size)
    q3 = qcat.reshape(Bd, MLA_HEADS, QC)
    new3 = rows_new.reshape(Bd, 1, MLA_CACHE_DIM)
    scratch = 2 * _nbytes((MLA_CACHE_DIM, L), F32) + _nbytes((MLA_KV_RANK, L), BF16)
    grid_spec = pltpu.PrefetchScalarGridSpec(
        num_scalar_prefetch=1,
        grid=(Bd,),
        in_specs=[pl.BlockSpec((1, MLA_HEADS, QC), lambda b, pt: (b, 0, 0)),
                  pl.BlockSpec((1, 1, MLA_CACHE_DIM), lambda b, pt: (b, 0, 0)),
                  pl.BlockSpec(memory_space=pl.ANY)],
        out_specs=pl.BlockSpec((1, MLA_HEADS, MLA_KV_RANK), lambda b, pt: (b, 0, 0)),
        scratch_shapes=[pltpu.VMEM((2, MLA_CACHE_DIM, L), F32),
                        pltpu.VMEM((MLA_KV_RANK, L), BF16),
                        pltpu.SemaphoreType.DMA((2,))],
    )
    return pl.pallas_call(
        functools.partial(_dec_attn_kernel, layer=layer, n_pages=n_pages, page_size=page_size,
                          chunk=chunk),
        grid_spec=grid_spec,
        out_shape=jax.ShapeDtypeStruct((Bd, MLA_HEADS, MLA_KV_RANK), F32),
        compiler_params=_params(1 << 20, scratch, ("arbitrary",)),
        name="mla_decode_attn",
    )(pt_flat, q3, new3, cache_t)


def _dec_out_kernel(o_ref, w_ref, y_ref):
    y_ref[...] = _dot(o_ref[...].astype(BF16), w_ref[...]).astype(y_ref.dtype)


def _dec_out(o_lat2d, wuv_bd, layer):
    Bd = o_lat2d.shape[0]
    return pl.pallas_call(
        _dec_out_kernel,
        grid=(MLA_HEADS // 2,),
        in_specs=[pl.BlockSpec((Bd, 2 * MLA_KV_RANK), lambda hp: (0, hp)),
                  pl.BlockSpec((None, None, 2 * MLA_KV_RANK, 2 * MLA_V), lambda hp: (layer, hp, 0, 0))],
        out_specs=pl.BlockSpec((Bd, 2 * MLA_V), lambda hp: (0, hp)),
        out_shape=jax.ShapeDtypeStruct((Bd, MLA_HEADS * MLA_V), BF16),
        name="mla_decode_out",
    )(o_lat2d, wuv_bd)


def _log_decay(gg, wg, bg):
    g_pre = _dot(gg.astype(BF16), wg) + bg
    return (jnp.minimum(g_pre, 0.0) - jnp.log1p(jnp.exp(-jnp.abs(g_pre)))) / GLA_TAU


def _head_out(o, g_o, gr):
    return _rmsn(o, g_o) * (gr * jax.nn.sigmoid(gr))


def _gla_masks():
    C = GLA_CHUNK
    row = lax.broadcasted_iota(jnp.int32, (C, C), 0)
    col = lax.broadcasted_iota(jnp.int32, (C, C), 1)
    level = jnp.zeros((C, C), jnp.int32)
    for n, size in enumerate(GLA_LEVELS):
        sh = size.bit_length() - 1
        rb, cb = row >> sh, col >> sh
        level = jnp.where(rb - cb == 1, jnp.where((rb & 1) == 1, n + 1, level), level)
    rel = col - ((row >> 3) << 3)
    return level, rel, row >= col


def _gla_scores(q, k, b, level, rel, causal):
    C, DK = q.shape
    a = jnp.zeros((C, C), F32)
    for n, size in enumerate(GLA_LEVELS):
        ref = jnp.concatenate(
            [jnp.broadcast_to(b[p + size - 1:p + size, :], (2 * size, DK))
             for p in range(0, C, 2 * size)], axis=0)
        e = jnp.exp(-jnp.abs(b - ref))
        a = jnp.where(level == n + 1, _dot_nt((q * e).astype(BF16), (k * e).astype(BF16)), a)
    G = C // GLA_DIAG
    k3 = k.reshape(G, GLA_DIAG, DK)
    b3 = b.reshape(G, GLA_DIAG, DK)
    for j in range(GLA_DIAG):
        kj = jnp.broadcast_to(k3[:, j:j + 1, :], (G, GLA_DIAG, DK)).reshape(C, DK)
        bj = jnp.broadcast_to(b3[:, j:j + 1, :], (G, GLA_DIAG, DK)).reshape(C, DK)
        w = q * kj * jnp.exp(b - bj)
        a = jnp.where(rel == j, jnp.sum(w, axis=-1, keepdims=True), a)
    return jnp.where(causal, a, 0.0)


def _gla_prompt_kernel(gq_ref, gk_ref, gv_ref, gr_ref, gg_ref, wg_ref, bg_ref, go_ref,
                       y_ref, s_ref, la_ref):
    T = gq_ref.shape[0]
    C = GLA_CHUNK
    la_ref[...] = _log_decay(gg_ref[...], wg_ref[...], bg_ref[...])
    level, rel, causal = _gla_masks()
    tri = causal.astype(F32)
    g_o = go_ref[...]

    def chunk(c, st):
        r0 = pl.multiple_of(c * C, C)
        q = gq_ref[pl.ds(r0, C), :] * (GLA_DK ** -0.5)
        k = gk_ref[pl.ds(r0, C), :]
        v = gv_ref[pl.ds(r0, C), :].astype(BF16)
        b = jnp.dot(tri, la_ref[pl.ds(r0, C), :], precision=lax.Precision.HIGHEST,
                    preferred_element_type=F32)
        b_last = b[C - 1:C, :]
        a = _gla_scores(q, k, b, level, rel, causal)
        o = _dot_nt((q * jnp.exp(b)).astype(BF16), st.astype(BF16)) + _dot(a.astype(BF16), v)
        y_ref[pl.ds(r0, C), :] = _head_out(o, g_o, gr_ref[pl.ds(r0, C), :]).astype(y_ref.dtype)
        kd = (k * jnp.exp(b_last - b)).astype(BF16)
        return st * jnp.exp(b_last) + _dot_tn(v, kd)

    st = lax.fori_loop(0, T // C, chunk, jnp.zeros((GLA_DV, GLA_DK), F32), unroll=2)
    s_ref[...] = st.T


def _gla_prompt(z, W, layer, B, T):
    DK, DV, H = GLA_DK, GLA_DV, GLA_HEADS
    blocks = (3 * _nbytes((T, DK), F32) + 2 * _nbytes((T, DV), F32) + _nbytes((T, DV), BF16)
              + _nbytes((DK, DV), F32))
    scratch = _nbytes((T, DK), F32)
    return pl.pallas_call(
        _gla_prompt_kernel,
        grid=(B, H),
        in_specs=[pl.BlockSpec((T, DK), lambda b, h: (b, Z_GQ // DK + h)),
                  pl.BlockSpec((T, DK), lambda b, h: (b, Z_GK // DK + h)),
                  pl.BlockSpec((T, DV), lambda b, h: (b, Z_GV // DV + h)),
                  pl.BlockSpec((T, DV), lambda b, h: (b, Z_GR // DV + h)),
                  pl.BlockSpec((T, LANES), lambda b, h: (b, Z_GG // LANES)),
                  pl.BlockSpec((None, LANES, DK), lambda b, h: (layer, 0, h)),
                  pl.BlockSpec((None, 1, DK), lambda b, h: (layer, 0, h)),
                  pl.BlockSpec((None, 1, DV), lambda b, h: (layer, 0, 0))],
        out_specs=[pl.BlockSpec((T, DV), lambda b, h: (b, h)),
                   pl.BlockSpec((None, None, DK, DV), lambda b, h: (b, h, 0, 0))],
        out_shape=[jax.ShapeDtypeStruct((B * T, H * DV), BF16),
                   jax.ShapeDtypeStruct((B, H, DK, DV), F32)],
        scratch_shapes=[pltpu.VMEM((T, DK), F32)],
        compiler_params=_params(blocks, scratch, ("parallel", "parallel")),
        name="gla_prompt",
    )(z, z, z, z, z, W["wg2"], W["b_gla_g"], W["g_gla_o"])


def _gla_decode_kernel(gq_ref, gk_ref, gv_ref, gr_ref, gg_ref, s_ref, wg_ref, bg_ref, go_ref,
                       y_ref, ns_ref, o_scr):
    nb = gq_ref.shape[0]
    DK, DV, H = GLA_DK, GLA_DV, GLA_HEADS
    la = _log_decay(gg_ref[...], wg_ref[...], bg_ref[...])
    a = jnp.exp(la)
    q = gq_ref[...] * (GLA_DK ** -0.5)
    k = gk_ref[...]
    v = gv_ref[...]
    qa = q * a
    pieces = [x[:, h * DK:(h + 1) * DK] for x in (a, k, qa) for h in range(H)]
    pad = LANES - 3 * H * nb
    stack = jnp.concatenate(pieces + [jnp.zeros((pad, DK), F32)], axis=0)
    cols = stack.T
    qk = q * k
    for h in range(H):
        att = jnp.sum(qk[:, h * DK:(h + 1) * DK], axis=-1, keepdims=True)
        o_scr[:, h * DV:(h + 1) * DV] = att * v[:, h * DV:(h + 1) * DV]
    for i in range(nb):
        for h in range(H):
            c = h * nb + i
            a_col = cols[:, c:c + 1]
            k_col = cols[:, H * nb + c:H * nb + c + 1]
            qa_col = cols[:, 2 * H * nb + c:2 * H * nb + c + 1]
            s_old = s_ref[i, h]
            v_row = v[i:i + 1, h * DV:(h + 1) * DV]
            ns_ref[i, h] = a_col * s_old + k_col * v_row
            o_scr[i:i + 1, h * DV:(h + 1) * DV] += jnp.sum(qa_col * s_old, axis=0, keepdims=True)
    g_o = go_ref[...]
    gr = gr_ref[...]
    for h in range(H):
        sl = slice(h * DV, (h + 1) * DV)
        y_ref[:, sl] = _head_out(o_scr[:, sl], g_o, gr[:, sl]).astype(y_ref.dtype)


def _gla_decode(z, state, W, layer):
    Bd = z.shape[0]
    DK, DV, H = GLA_DK, GLA_DV, GLA_HEADS
    nb = 8
    HK, HV = H * DK, H * DV
    blocks = (2 * _nbytes((nb, HK), F32) + 3 * _nbytes((nb, HV), F32) + _nbytes((nb, LANES), F32)
              + 2 * _nbytes((nb, H, DK, DV), F32) + _nbytes((LANES, HK), BF16))
    return pl.pallas_call(
        _gla_decode_kernel,
        grid=(Bd // nb,),
        in_specs=[pl.BlockSpec((nb, HK), lambda i: (i, Z_GQ // HK)),
                  pl.BlockSpec((nb, HK), lambda i: (i, Z_GK // HK)),
                  pl.BlockSpec((nb, HV), lambda i: (i, Z_GV // HV)),
                  pl.BlockSpec((nb, HV), lambda i: (i, Z_GR // HV)),
                  pl.BlockSpec((nb, LANES), lambda i: (i, Z_GG // LANES)),
                  pl.BlockSpec((None, nb, H, DK, DV), lambda i: (layer, i, 0, 0, 0)),
                  pl.BlockSpec((None, LANES, HK), lambda i: (layer, 0, 0)),
                  pl.BlockSpec((None, 1, HK), lambda i: (layer, 0, 0)),
                  pl.BlockSpec((None, 1, DV), lambda i: (layer, 0, 0))],
        out_specs=[pl.BlockSpec((nb, HV), lambda i: (i, 0)),
                   pl.BlockSpec((nb, H, DK, DV), lambda i: (i, 0, 0, 0))],
        out_shape=[jax.ShapeDtypeStruct((Bd, HV), F32),
                   jax.ShapeDtypeStruct((Bd, H, DK, DV), F32)],
        scratch_shapes=[pltpu.VMEM((nb, HV), F32)],
        compiler_params=_params(blocks, 0, ("parallel",)),
        name="gla_decode",
    )(z, z, z, z, z, state, W["wg2"], W["b_gla_g"], W["g_gla_o"])


def _merge_kernel(yc_ref, ym_ref, yg_ref, gc_ref, gm_ref, gg_ref, wc_ref, wm_ref, wg_ref, o_ref):
    yc = yc_ref[...].astype(BF16)
    ym = ym_ref[...].astype(BF16)
    yg = yg_ref[...].astype(BF16)
    for c in range(0, o_ref.shape[1], MXU_COLS):
        sl = slice(c, c + MXU_COLS)
        acc = jax.nn.sigmoid(gc_ref[:, sl]) * _dot(yc, wc_ref[:, sl])
        acc = acc + jax.nn.sigmoid(gm_ref[:, sl]) * _dot(ym, wm_ref[:, sl])
        acc = acc + jax.nn.sigmoid(gg_ref[:, sl]) * _dot(yg, wg_ref[:, sl])
        o_ref[:, sl] = acc.astype(o_ref.dtype)


def _merge(y_conv, y_mla, y_gla, z, W, layer):
    M = z.shape[0]
    tm = _pick(M, 1024, 16)
    tn = 512
    C = y_conv.shape[1]
    ga = Z_GA // tn
    gstep = D_MODEL // tn
    blocks = (3 * _nbytes((tm, C), F32) + 3 * _nbytes((tm, tn), F32) + 3 * _nbytes((C, tn), BF16)
              + _nbytes((tm, tn), BF16))
    y_spec = pl.BlockSpec((tm, C), lambda i, j: (i, 0))
    w_spec = pl.BlockSpec((None, C, tn), lambda i, j: (layer, 0, j))
    return pl.pallas_call(
        _merge_kernel,
        grid=(M // tm, D_MODEL // tn),
        in_specs=[y_spec, y_spec, y_spec,
                  pl.BlockSpec((tm, tn), lambda i, j: (i, ga + j)),
                  pl.BlockSpec((tm, tn), lambda i, j: (i, ga + gstep + j)),
                  pl.BlockSpec((tm, tn), lambda i, j: (i, ga + 2 * gstep + j)),
                  w_spec, w_spec, w_spec],
        out_specs=pl.BlockSpec((tm, tn), lambda i, j: (i, j)),
        out_shape=jax.ShapeDtypeStruct((M, D_MODEL), BF16),
        compiler_params=_params(blocks, 0, ("parallel", "arbitrary")),
        name="merge",
    )(y_conv, y_mla, y_gla, z, z, z, W["w_br_conv"], W["w_br_mla"], W["w_br_gla"])


def _ffn_up_kernel(h_ref, halo_ref, g_ref, wg_ref, wv_ref, cg_ref, cv_ref,
                   act_ref, sg_ref, sv_ref, xn_ref, *, tiles_per_seq):
    tm = h_ref.shape[0]
    i = pl.program_id(0)

    @pl.when(pl.program_id(1) == 0)
    def _():
        g = g_ref[...]
        halo = _rmsn(halo_ref[...], g)
        halo = jnp.where(i % tiles_per_seq == 0, 0.0, halo)
        xn_ref[:HALO, :] = halo.astype(BF16)
        xn_ref[HALO:, :] = _rmsn(h_ref[...], g).astype(BF16)

    xn = xn_ref[...]

    def conv(u, c_ref, sl):
        out = pltpu.roll(u, 2, axis=0) * c_ref[0:1, sl]
        out = out + pltpu.roll(u, 1, axis=0) * c_ref[1:2, sl]
        out = out + u * c_ref[2:3, sl]
        return out[HALO:]

    for c in range(0, act_ref.shape[1], MXU_COLS):
        sl = slice(c, c + MXU_COLS)
        ug = _dot(xn, wg_ref[:, sl])
        uv = _dot(xn, wv_ref[:, sl])
        gate = conv(ug, cg_ref, sl)
        val = conv(uv, cv_ref, sl)
        act_ref[:, sl] = (gate * jax.nn.sigmoid(gate) * val).astype(act_ref.dtype)
        sg_ref[:, sl] = ug[HALO + tm - 2:, :]
        sv_ref[:, sl] = uv[HALO + tm - 2:, :]


def _ffn_up_prompt(h, W, layer, B, T):
    M, K = h.shape
    tm = _pick(T, 1024, HALO)
    tn = _pick(D_FF, 512, LANES)
    nt = T // tm
    nj = D_FF // tn
    blocks = (_nbytes((tm + HALO, K), F32) + 2 * _nbytes((K, tn), BF16) + _nbytes((tm, tn), BF16)
              + 4 * _nbytes((tm + HALO, tn), F32))
    scratch = _nbytes((tm + HALO, K), BF16)
    hb = tm // HALO
    act, sg, sv = pl.pallas_call(
        functools.partial(_ffn_up_kernel, tiles_per_seq=nt),
        grid=(M // tm, nj),
        in_specs=[pl.BlockSpec((tm, K), lambda i, j: (i, 0)),
                  pl.BlockSpec((HALO, K), lambda i, j: (jnp.maximum(i * hb - 1, 0), 0)),
                  pl.BlockSpec((None, 1, K), lambda i, j: (layer, 0, 0)),
                  pl.BlockSpec((None, K, tn), lambda i, j: (layer, 0, j)),
                  pl.BlockSpec((None, K, tn), lambda i, j: (layer, 0, nj + j)),
                  pl.BlockSpec((None, 3, tn), lambda i, j: (layer, 0, j)),
                  pl.BlockSpec((None, 3, tn), lambda i, j: (layer, 0, nj + j))],
        out_specs=[pl.BlockSpec((tm, tn), lambda i, j: (i, j)),
                   pl.BlockSpec((None, 2, tn), lambda i, j: (i, 0, j)),
                   pl.BlockSpec((None, 2, tn), lambda i, j: (i, 0, j))],
        out_shape=[jax.ShapeDtypeStruct((M, D_FF), BF16),
                   jax.ShapeDtypeStruct((M // tm, 2, D_FF), F32),
                   jax.ShapeDtypeStruct((M // tm, 2, D_FF), F32)],
        scratch_shapes=[pltpu.VMEM((tm + HALO, K), BF16)],
        compiler_params=_params(blocks, scratch, ("parallel", "arbitrary")),
        name="ffn_up",
    )(h, h, W["g_ffn"], W["w_up"], W["w_up"], W["w_ffn_conv"], W["w_ffn_conv"])
    return act, sg[nt - 1::nt], sv[nt - 1::nt]


def _ffn_conv_decode_kernel(ug_ref, uv_ref, s0g_ref, s0v_ref, s1g_ref, s1v_ref, cg_ref, cv_ref,
                            act_ref):
    gate = s0g_ref[...] * cg_ref[0:1, :] + s1g_ref[...] * cg_ref[1:2, :] + ug_ref[...] * cg_ref[2:3, :]
    val = s0v_ref[...] * cv_ref[0:1, :] + s1v_ref[...] * cv_ref[1:2, :] + uv_ref[...] * cv_ref[2:3, :]
    act_ref[...] = (gate * jax.nn.sigmoid(gate) * val).astype(act_ref.dtype)


def _ffn_conv_decode(u, state2d, w_ffn_conv, layer):
    Bd = u.shape[0]
    tn = _pick(D_FF, 1024, LANES)
    nj = D_FF // tn
    blocks = 6 * _nbytes((Bd, tn), F32) + _nbytes((Bd, tn), BF16)
    u_spec = lambda off: pl.BlockSpec((Bd, tn), lambda j: (0, off + j))
    s_spec = lambda off: pl.BlockSpec((None, Bd, tn), lambda j: (layer, 0, off + j))
    c_spec = lambda off: pl.BlockSpec((None, 3, tn), lambda j: (layer, 0, off + j))
    return pl.pallas_call(
        _ffn_conv_decode_kernel,
        grid=(nj,),
        in_specs=[u_spec(0), u_spec(nj), s_spec(0), s_spec(nj), s_spec(2 * nj), s_spec(3 * nj),
                  c_spec(0), c_spec(nj)],
        out_specs=pl.BlockSpec((Bd, tn), lambda j: (0, j)),
        out_shape=jax.ShapeDtypeStruct((Bd, D_FF), BF16),
        compiler_params=_params(blocks, 0, ("parallel",)),
        name="ffn_conv_decode",
    )(u, u, state2d, state2d, state2d, state2d, w_ffn_conv, w_ffn_conv)


def _ple_kernel(h_ref, g_ref, wg_ref, p_ref, wp_ref, o_ref, xn_ref):
    tn = o_ref.shape[1]
    j = pl.program_id(1)

    @pl.when(j == 0)
    def _():
        xn_ref[...] = _rmsn(h_ref[...], g_ref[...]).astype(BF16)

    xn = xn_ref[...]
    p = p_ref[...].astype(BF16)
    for c in range(0, tn, MXU_COLS):
        sl = slice(c, c + MXU_COLS)
        gate = jax.nn.sigmoid(_dot(xn, wg_ref[:, sl]))
        emb = _dot(p, wp_ref[:, sl])
        res = h_ref[:, pl.ds(pl.multiple_of(j * tn + c, MXU_COLS), MXU_COLS)]
        o_ref[:, sl] = res + gate * emb


def _ple(h, p, W, layer):
    M, K = h.shape
    tm = _pick(M, 1024, 16)
    tn = 512
    blocks = (_nbytes((tm, K), F32) + _nbytes((K, tn), BF16) + _nbytes((tm, PLE_DIM), F32)
              + _nbytes((PLE_DIM, tn), BF16) + _nbytes((tm, tn), F32))
    return pl.pallas_call(
        _ple_kernel,
        grid=(M // tm, K // tn),
        in_specs=[pl.BlockSpec((tm, K), lambda i, j: (i, 0)),
                  pl.BlockSpec((None, 1, K), lambda i, j: (layer, 0, 0)),
                  pl.BlockSpec((None, K, tn), lambda i, j: (layer, 0, j)),
                  pl.BlockSpec((None, tm, PLE_DIM), lambda i, j: (layer, i, 0)),
                  pl.BlockSpec((None, PLE_DIM, tn), lambda i, j: (layer, 0, j))],
        out_specs=pl.BlockSpec((tm, tn), lambda i, j: (i, j)),
        out_shape=jax.ShapeDtypeStruct((M, K), F32),
        scratch_shapes=[pltpu.VMEM((tm, K), BF16)],
        compiler_params=_params(blocks, _nbytes((tm, K), BF16), ("parallel", "arbitrary")),
        name="ple",
    )(h, W["g_ple"], W["w_ple_gate"], p, W["w_ple"])


def _final_norm_kernel(h_ref, g_ref, o_ref):
    o_ref[...] = _rmsn(h_ref[...], g_ref[...])


def _final_norm(h, g):
    M, K = h.shape
    tm = _pick(M, 512)
    return pl.pallas_call(
        _final_norm_kernel,
        grid=(M // tm,),
        in_specs=[pl.BlockSpec((tm, K), lambda i: (i, 0)),
                  pl.BlockSpec((1, K), lambda i: (0, 0))],
        out_specs=pl.BlockSpec((tm, K), lambda i: (i, 0)),
        out_shape=jax.ShapeDtypeStruct((M, K), F32),
        compiler_params=_params(2 * _nbytes((tm, K), F32), 0, ("parallel",)),
        name="final_norm",
    )(h, g)


def _prep_weights(w):
    depth = w["w_in"].shape[0]
    H, R = MLA_HEADS, MLA_KV_RANK
    offs = [0]
    for s in IN_SPLITS:
        offs.append(offs[-1] + s)
    seg = lambda n: w["w_in"][:, :, offs[n]:offs[n + 1]]
    zeros = lambda n: jnp.zeros((depth, D_MODEL, n), F32)
    w_in = jnp.concatenate(
        [seg(0), seg(1), seg(2), seg(10), seg(11), seg(12), seg(7), seg(9), seg(3), seg(5), seg(6),
         seg(4), zeros(3 * LANES - MLA_CACHE_DIM), seg(8), zeros(LANES - GLA_GATE_RANK)],
        axis=-1).astype(BF16)
    assert w_in.shape[-1] == Z_DIM

    wq = w["w_qb"].reshape(depth, MLA_Q_RANK, H, MLA_QK)
    half = MLA_ROPE // 2
    x1 = wq[..., MLA_NOPE:MLA_NOPE + half]
    x2 = wq[..., MLA_NOPE + half:]
    pad_q = jnp.zeros((depth, MLA_Q_RANK, H, HEAD_PAD - MLA_QK), F32)
    wq_p = jnp.concatenate([wq, pad_q], axis=-1)
    wq_sw = jnp.concatenate([jnp.zeros_like(wq[..., :MLA_NOPE]), -x2, x1, pad_q], axis=-1)
    wq2 = jnp.concatenate([wq_p.reshape(depth, MLA_Q_RANK, H * HEAD_PAD),
                           wq_sw.reshape(depth, MLA_Q_RANK, H * HEAD_PAD)], axis=-1).astype(BF16)

    wkv = w["w_kvb"].reshape(depth, R, H, MLA_NOPE + MLA_V)
    w_uk, w_uv = wkv[..., :MLA_NOPE], wkv[..., MLA_NOPE:]
    wk_top = jnp.concatenate([w_uk, jnp.zeros((depth, R, H, HEAD_PAD - MLA_NOPE), F32)], axis=-1)
    eye = jnp.eye(MLA_ROPE, dtype=F32)
    copy = jnp.concatenate([jnp.zeros((MLA_ROPE, MLA_NOPE), F32), eye,
                            jnp.zeros((MLA_ROPE, HEAD_PAD - MLA_QK), F32)], axis=-1)
    copy = jnp.broadcast_to(copy[:, None, :], (MLA_ROPE, H, HEAD_PAD))
    wk_bot = jnp.concatenate([copy, jnp.zeros((LANES - MLA_ROPE, H, HEAD_PAD), F32)], axis=0)
    wk = jnp.concatenate([wk_top, jnp.broadcast_to(wk_bot[None], (depth, LANES, H, HEAD_PAD))],
                         axis=1).reshape(depth, R + LANES, H * HEAD_PAD).astype(BF16)
    wv = w_uv.reshape(depth, R, H * MLA_V).astype(BF16)

    QC = R + LANES
    uk_t = jnp.transpose(w_uk, (0, 2, 3, 1))
    top = jnp.concatenate([uk_t, jnp.zeros((depth, H, MLA_NOPE, LANES), F32)], axis=-1)
    mid = jnp.concatenate([jnp.zeros((MLA_ROPE, R), F32), eye,
                           jnp.zeros((MLA_ROPE, LANES - MLA_ROPE), F32)], axis=-1)
    mid = jnp.broadcast_to(mid[None, None], (depth, H, MLA_ROPE, QC))
    bot = jnp.zeros((depth, H, HEAD_PAD - MLA_QK, QC), F32)
    wabs = jnp.concatenate([top, mid, bot], axis=2).astype(BF16)
    uv_h = jnp.transpose(w_uv, (0, 2, 1, 3)).reshape(depth, H // 2, 2, R, MLA_V)
    zed = jnp.zeros_like(uv_h[:, :, 0])
    wuv_bd = jnp.concatenate([jnp.concatenate([uv_h[:, :, 0], zed], axis=-1),
                              jnp.concatenate([zed, uv_h[:, :, 1]], axis=-1)], axis=2).astype(BF16)

    wg2 = jnp.concatenate([w["w_gla_g2"],
                           jnp.zeros((depth, LANES - GLA_GATE_RANK, GLA_HEADS * GLA_DK), F32)],
                          axis=1).astype(BF16)
    row = lambda a: a[:, None, :]
    out = dict(w_in=w_in, wq2=wq2, wk=wk, wv=wv, wabs=wabs, wuv_bd=wuv_bd, wg2=wg2,
               g_mix=row(w["g_mix"]), g_qa=row(w["g_qa"]), g_kva=row(w["g_kva"]),
               b_gla_g=row(w["b_gla_g"]), g_gla_o=row(w["g_gla_o"]), g_ffn=row(w["g_ffn"]),
               g_ple=row(w["g_ple"]), w_conv=w["w_conv"], w_ffn_conv=w["w_ffn_conv"])
    for name in ("w_br_conv", "w_br_mla", "w_br_gla", "w_o", "w_up", "w_down", "w_ple_gate", "w_ple"):
        out[name] = w[name].astype(BF16)
    return out


def _rope_tables(pos):
    half = MLA_ROPE // 2
    inv = ROPE_THETA ** (-jnp.arange(half, dtype=F32) / half)
    ang = pos.astype(F32)[:, None] * inv[None, :]
    cos, sin = jnp.cos(ang), jnp.sin(ang)
    n = pos.shape[0]
    ones = jnp.ones((n, MLA_NOPE), F32)
    zq = jnp.zeros((n, HEAD_PAD - MLA_QK), F32)
    zk = jnp.zeros((n, LANES - MLA_ROPE), F32)
    c_q = jnp.concatenate([ones, cos, cos, zq], axis=1)
    s_q = jnp.concatenate([jnp.zeros_like(ones), sin, sin, zq], axis=1)
    c_k = jnp.concatenate([cos, cos, zk], axis=1)
    s_k = jnp.concatenate([-sin, sin, zk], axis=1)
    return c_q, s_q, c_k, s_k


def _layer_tail(h, z, y_conv, y_mla, y_gla, W, layer):
    merged = _merge(y_conv, y_mla, y_gla, z, W, layer)
    return _matmul_res(merged, W["w_o"], layer, h, tm_pref=512, tn_pref=2048, name="out_proj")


def _prompt_trunk(x, p, W, g_final):
    B, T, D = x.shape
    depth = W["w_in"].shape[0]
    M = B * T
    h = x.reshape(M, D)
    p2 = p.reshape(depth, M, PLE_DIM)
    tabs = _rope_tables(jnp.arange(T, dtype=jnp.int32))
    rows, conv_st, gla_st, ffn_st = [], [], [], []
    for i in range(depth):
        z = _rms_matmul(h, W["g_mix"], W["w_in"], i, tm_pref=1024, tn_pref=1024, name="in_proj")
        y_conv, cst = _conv_prompt(z, W["w_conv"], i, B, T)
        q, mla_rows, k, v = _mla_prep(z, W, i, tabs, T, with_kv=True)
        y_mla = _mla_attn(q, k, v, B, T)
        y_gla, s_new = _gla_prompt(z, W, i, B, T)
        h = _layer_tail(h, z, y_conv, y_mla, y_gla, W, i)
        act, sg, sv = _ffn_up_prompt(h, W, i, B, T)
        h = _matmul_res(act, W["w_down"], i, h, tm_pref=512, tn_pref=1024, name="ffn_down")
        h = _ple(h, p2, W, i)
        rows.append(mla_rows.reshape(B, T, MLA_CACHE_DIM))
        conv_st.append(cst)
        gla_st.append(s_new)
        ffn_st.append(jnp.concatenate([sg, sv], axis=-1))
    y = _final_norm(h, g_final).reshape(B, T, D)
    return y, jnp.stack(rows), jnp.stack(conv_st), jnp.stack(gla_st), jnp.stack(ffn_st)


def _decode_trunk(x, p, W, g_final, cache, page_table, st_conv, st_gla, st_ffn):
    Bd, T, D = x.shape
    assert T == 1, "decode path handles one new token per request"
    depth = W["w_in"].shape[0]
    n_pages = page_table.shape[1]
    past_len = n_pages * cache.shape[2]
    h = x.reshape(Bd, D)
    p2 = p.reshape(depth, Bd, PLE_DIM)
    tabs = _rope_tables(jnp.full((Bd,), past_len, dtype=jnp.int32))
    pt_flat = page_table.reshape(-1)
    cache_t = jnp.swapaxes(cache, 2, 3)
    st_conv2 = st_conv.reshape(depth, Bd, 2 * CONV_DIM)
    st_ffn2 = st_ffn.reshape(depth, Bd, 4 * D_FF)
    rows, conv_st, gla_st, ffn_st = [], [], [], []
    for i in range(depth):
        z = _rms_matmul(h, W["g_mix"], W["w_in"], i, tm_pref=1024, tn_pref=1024, name="in_proj_decode")
        y_conv, cst = _conv_decode(z, st_conv2, W["w_conv"], i)
        q, mla_rows = _mla_prep(z, W, i, tabs, Bd, with_kv=False)
        qcat = _absorb(q, W["wabs"], i)
        o_lat = _dec_attn(pt_flat, qcat, mla_rows, cache_t, i, n_pages)
        y_mla = _dec_out(o_lat.reshape(Bd, MLA_HEADS * MLA_KV_RANK), W["wuv_bd"], i)
        y_gla, s_new = _gla_decode(z, st_gla, W, i)
        h = _layer_tail(h, z, y_conv, y_mla, y_gla, W, i)
        u = _rms_matmul(h, W["g_ffn"], W["w_up"], i, tm_pref=1024, tn_pref=1024, name="ffn_up_decode")
        act = _ffn_conv_decode(u, st_ffn2, W["w_ffn_conv"], i)
        h = _matmul_res(act, W["w_down"], i, h, tm_pref=512, tn_pref=1024, name="ffn_down_decode")
        h = _ple(h, p2, W, i)
        rows.append(mla_rows.reshape(Bd, 1, MLA_CACHE_DIM))
        conv_st.append(cst.reshape(Bd, 2, CONV_DIM))
        gla_st.append(s_new)
        ffn_st.append(jnp.stack([st_ffn[i, :, 1, :], u], axis=1))
    y = _final_norm(h, g_final).reshape(Bd, 1, D)
    return y, jnp.stack(rows), jnp.stack(conv_st), jnp.stack(gla_st), jnp.stack(ffn_st)


def kernel(x_prompt, x_sample, cache_mla, state_conv, state_gla, state_ffn, page_table, p_prompt, p_sample, g_mix, w_in, w_conv, g_qa, w_qb, g_kva, w_kvb, w_gla_g2, b_gla_g, g_gla_o, w_br_conv, w_br_mla, w_br_gla, w_o, g_ffn, w_up, w_ffn_conv, w_down, g_ple, w_ple_gate, w_ple, g_final):
    W = _prep_weights(dict(
        g_mix=g_mix, w_in=w_in, w_conv=w_conv, g_qa=g_qa, w_qb=w_qb, g_kva=g_kva, w_kvb=w_kvb,
        w_gla_g2=w_gla_g2, b_gla_g=b_gla_g, g_gla_o=g_gla_o, w_br_conv=w_br_conv,
        w_br_mla=w_br_mla, w_br_gla=w_br_gla, w_o=w_o, g_ffn=g_ffn, w_up=w_up,
        w_ffn_conv=w_ffn_conv, w_down=w_down, g_ple=g_ple, w_ple_gate=w_ple_gate, w_ple=w_ple))
    g_fin = g_final[None, :]
    y_p, mla_p, conv_p, gla_p, ffn_p = _prompt_trunk(x_prompt, p_prompt, W, g_fin)
    y_s, mla_s, conv_s, gla_s, ffn_s = _decode_trunk(
        x_sample, p_sample, W, g_fin, cache_mla, page_table, state_conv, state_gla, state_ffn)
    return (y_p, y_s, mla_p, mla_s, conv_p, conv_s, gla_p, gla_s, ffn_p, ffn_s)
```

```python
import functools

import jax
import jax.numpy as jnp
from jax import lax
from jax.experimental import pallas as pl
from jax.experimental.pallas import tpu as pltpu

F32 = jnp.float32
BF16 = jnp.bfloat16

D_MODEL = 2048
CONV_DIM = D_MODEL // 2
MLA_HEADS = 16
MLA_NOPE = 64
MLA_ROPE = 32
MLA_QK = MLA_NOPE + MLA_ROPE
MLA_V = 64
MLA_Q_RANK = D_MODEL // 4
MLA_KV_RANK = D_MODEL // 8
MLA_CACHE_DIM = MLA_KV_RANK + MLA_ROPE
ROPE_THETA = 10000.0
GLA_HEADS = 4
GLA_DK = D_MODEL // 16
GLA_DV = D_MODEL // 8
GLA_GATE_RANK = 16
GLA_TAU = 16.0
GLA_CHUNK = 64
GLA_LEVELS = (32, 16, 8)
GLA_DIAG = 8
D_FF = ((8 * D_MODEL // 3 + 255) // 256) * 256
PLE_DIM = 256
EPS = 1e-6

IN_SPLITS = (CONV_DIM, CONV_DIM, CONV_DIM, MLA_Q_RANK, MLA_CACHE_DIM,
             GLA_HEADS * GLA_DK, GLA_HEADS * GLA_DK, GLA_HEADS * GLA_DV, GLA_GATE_RANK,
             GLA_HEADS * GLA_DV, D_MODEL, D_MODEL, D_MODEL)

LANES = 128
HEAD_PAD = LANES
VMEM_CAP = 60 * 1024 * 1024
HALO = 16
ROW_SUB = 256

Z_CB, Z_CC, Z_CH = 0, CONV_DIM, 2 * CONV_DIM
Z_GA = 3 * CONV_DIM
Z_GV = Z_GA + 3 * D_MODEL
Z_GR = Z_GV + GLA_HEADS * GLA_DV
Z_QA = Z_GR + GLA_HEADS * GLA_DV
Z_GQ = Z_QA + MLA_Q_RANK
Z_GK = Z_GQ + GLA_HEADS * GLA_DK
Z_KVG = Z_GK + GLA_HEADS * GLA_DK
KVG_W = 4 * LANES
Z_GG = Z_KVG + 3 * LANES
Z_DIM = Z_KVG + KVG_W


def _pick(n, pref, mult=8):
    if n <= pref:
        return n
    for t in range(pref, 0, -1):
        if n % t == 0 and t % mult == 0:
            return t
    raise ValueError(f"no tile for {n}")


def _params(block_bytes, scratch_bytes=0, sem=None):
    need = 2 * block_bytes + scratch_bytes + (8 << 20)
    kw = dict(vmem_limit_bytes=int(min(max(need, 32 << 20), VMEM_CAP)))
    if sem is not None:
        kw["dimension_semantics"] = sem
    return pltpu.CompilerParams(**kw)


def _nbytes(shape, dtype):
    n = 1
    for s in shape:
        n *= s
    return n * jnp.dtype(dtype).itemsize


def _rmsn(x, g):
    y = x * lax.rsqrt(jnp.mean(x * x, axis=-1, keepdims=True) + EPS)
    return y * g


def _dot(a, b):
    return jnp.dot(a, b, preferred_element_type=F32)


def _dot_nt(a, b):
    return lax.dot_general(a, b, (((1,), (1,)), ((), ())), preferred_element_type=F32)


def _dot_tn(a, b):
    return lax.dot_general(a, b, (((0,), (0,)), ((), ())), preferred_element_type=F32)


def _rms_mm_kernel(x_ref, g_ref, w_ref, o_ref, xn_ref):
    @pl.when(pl.program_id(1) == 0)
    def _():
        xn_ref[...] = _rmsn(x_ref[...], g_ref[...]).astype(BF16)

    o_ref[...] = _dot(xn_ref[...], w_ref[...])


def _rms_matmul(x, g, w, layer, *, tm_pref, tn_pref, name):
    M, K = x.shape
    N = w.shape[-1]
    tm = _pick(M, tm_pref)
    tn = _pick(N, tn_pref, LANES)
    blocks = _nbytes((tm, K), F32) + _nbytes((K, tn), BF16) + _nbytes((tm, tn), F32)
    return pl.pallas_call(
        _rms_mm_kernel,
        grid=(M // tm, N // tn),
        in_specs=[pl.BlockSpec((tm, K), lambda i, j: (i, 0)),
                  pl.BlockSpec((None, 1, K), lambda i, j: (layer, 0, 0)),
                  pl.BlockSpec((None, K, tn), lambda i, j: (layer, 0, j))],
        out_specs=pl.BlockSpec((tm, tn), lambda i, j: (i, j)),
        out_shape=jax.ShapeDtypeStruct((M, N), F32),
        scratch_shapes=[pltpu.VMEM((tm, K), BF16)],
        compiler_params=_params(blocks, _nbytes((tm, K), BF16), ("parallel", "arbitrary")),
        name=name,
    )(x, g, w)


def _mm_res_kernel(x_ref, w_ref, r_ref, o_ref):
    o_ref[...] = r_ref[...] + _dot(x_ref[...].astype(BF16), w_ref[...])


def _matmul_res(x, w, layer, res, *, tm_pref, tn_pref, name):
    M, K = x.shape
    N = w.shape[-1]
    tm = _pick(M, tm_pref, 16)
    tn = _pick(N, tn_pref, LANES)
    blocks = _nbytes((tm, K), x.dtype) + _nbytes((K, tn), BF16) + 2 * _nbytes((tm, tn), F32)
    return pl.pallas_call(
        _mm_res_kernel,
        grid=(N // tn, M // tm),
        in_specs=[pl.BlockSpec((tm, K), lambda j, i: (i, 0)),
                  pl.BlockSpec((None, K, tn), lambda j, i: (layer, 0, j)),
                  pl.BlockSpec((tm, tn), lambda j, i: (i, j))],
        out_specs=pl.BlockSpec((tm, tn), lambda j, i: (i, j)),
        out_shape=jax.ShapeDtypeStruct((M, N), F32),
        compiler_params=_params(blocks, 0, ("parallel", "parallel")),
        name=name,
    )(x, w, res)


def _shift_rows(x, k):
    rows = lax.broadcasted_iota(jnp.int32, x.shape, 0)
    return jnp.where(rows >= k, pltpu.roll(x, k, axis=0), 0.0)


def _conv_prompt_kernel(cb_ref, cc_ref, ch_ref, w_ref, y_ref, st_ref):
    T = cb_ref.shape[0]
    x = cc_ref[...] * ch_ref[...]
    zc = _shift_rows(x, 2) * w_ref[0:1, :] + _shift_rows(x, 1) * w_ref[1:2, :] + x * w_ref[2:3, :]
    y_ref[...] = (cb_ref[...] * zc).astype(y_ref.dtype)
    st_ref[...] = x[T - 2:T, :]


def _conv_prompt(z, w_conv, layer, B, T):
    tc = 2 * LANES
    nc = CONV_DIM // tc
    blocks = 3 * _nbytes((T, tc), F32) + _nbytes((T, tc), BF16)
    return pl.pallas_call(
        _conv_prompt_kernel,
        grid=(B, nc),
        in_specs=[pl.BlockSpec((T, tc), lambda b, c: (b, Z_CB // tc + c)),
                  pl.BlockSpec((T, tc), lambda b, c: (b, Z_CC // tc + c)),
                  pl.BlockSpec((T, tc), lambda b, c: (b, Z_CH // tc + c)),
                  pl.BlockSpec((None, 3, tc), lambda b, c: (layer, 0, c))],
        out_specs=[pl.BlockSpec((T, tc), lambda b, c: (b, c)),
                   pl.BlockSpec((None, 2, tc), lambda b, c: (b, 0, c))],
        out_shape=[jax.ShapeDtypeStruct((B * T, CONV_DIM), BF16),
                   jax.ShapeDtypeStruct((B, 2, CONV_DIM), F32)],
        compiler_params=_params(blocks, 0, ("parallel", "parallel")),
        name="conv_prompt",
    )(z, z, z, w_conv)


def _conv_decode_kernel(cb_ref, cc_ref, ch_ref, st_ref, w_ref, y_ref, nst_ref):
    C = cb_ref.shape[1]
    x = cc_ref[...] * ch_ref[...]
    s0 = st_ref[:, :C]
    s1 = st_ref[:, C:]
    zc = s0 * w_ref[0:1, :] + s1 * w_ref[1:2, :] + x * w_ref[2:3, :]
    y_ref[...] = (cb_ref[...] * zc).astype(y_ref.dtype)
    nst_ref[:, :C] = s1
    nst_ref[:, C:] = x


def _conv_decode(z, state2d, w_conv, layer):
    Bd = z.shape[0]
    C = CONV_DIM
    blocks = 3 * _nbytes((Bd, C), F32) + 4 * _nbytes((Bd, C), F32) + _nbytes((Bd, C), BF16)
    return pl.pallas_call(
        _conv_decode_kernel,
        grid=(1,),
        in_specs=[pl.BlockSpec((Bd, C), lambda i: (0, Z_CB // C)),
                  pl.BlockSpec((Bd, C), lambda i: (0, Z_CC // C)),
                  pl.BlockSpec((Bd, C), lambda i: (0, Z_CH // C)),
                  pl.BlockSpec((None, Bd, 2 * C), lambda i: (layer, 0, 0)),
                  pl.BlockSpec((None, 3, C), lambda i: (layer, 0, 0))],
        out_specs=[pl.BlockSpec((Bd, C), lambda i: (0, 0)),
                   pl.BlockSpec((Bd, 2 * C), lambda i: (0, 0))],
        out_shape=[jax.ShapeDtypeStruct((Bd, C), BF16),
                   jax.ShapeDtypeStruct((Bd, 2 * C), F32)],
        compiler_params=_params(blocks),
        name="conv_decode",
    )(z, z, z, state2d, w_conv)


def _mla_prep_kernel(qa_ref, kvg_ref, gqa_ref, gkv_ref, wq_ref, wk_ref, wv_ref,
                     cq_ref, sq_ref, ck_ref, sk_ref, *out_refs, with_kv):
    q_ref, rows_ref = out_refs[0], out_refs[1]
    HP = MLA_HEADS * HEAD_PAD
    cq = _rmsn(qa_ref[...], gqa_ref[...]).astype(BF16)
    qq = _dot(cq, wq_ref[...])
    c_q, s_q = cq_ref[...], sq_ref[...]
    for h in range(MLA_HEADS):
        lo = h * HEAD_PAD
        q_ref[:, lo:lo + HEAD_PAD] = (qq[:, lo:lo + HEAD_PAD] * c_q
                                      + qq[:, HP + lo:HP + lo + HEAD_PAD] * s_q).astype(q_ref.dtype)
    ckv = _rmsn(kvg_ref[:, :MLA_KV_RANK], gkv_ref[...])
    kr = kvg_ref[:, MLA_KV_RANK:MLA_KV_RANK + LANES]
    lane = lax.broadcasted_iota(jnp.int32, kr.shape, 1)
    half = MLA_ROPE // 2
    partner = jnp.where(lane < half, pltpu.roll(kr, LANES - half, axis=1), pltpu.roll(kr, half, axis=1))
    kr = kr * ck_ref[...] + partner * sk_ref[...]
    rows_ref[:, :MLA_KV_RANK] = ckv
    rows_ref[:, MLA_KV_RANK:] = kr[:, :MLA_ROPE]
    if with_kv:
        k_ref, v_ref = out_refs[2], out_refs[3]
        ckv_b = ckv.astype(BF16)
        kin = jnp.concatenate([ckv_b, kr.astype(BF16)], axis=1)
        k_ref[...] = _dot(kin, wk_ref[...]).astype(k_ref.dtype)
        v_ref[...] = _dot(ckv_b, wv_ref[...]).astype(v_ref.dtype)


def _mla_prep(z, W, layer, tabs, T, *, with_kv):
    M = z.shape[0]
    tm = _pick(min(M, T), 512, 16)
    nt = max(T // tm, 1)
    HP = MLA_HEADS * HEAD_PAD
    HV = MLA_HEADS * MLA_V
    KIN = MLA_KV_RANK + LANES
    tab_spec = pl.BlockSpec((tm, LANES), lambda i: (i % nt, 0))
    out_specs = [pl.BlockSpec((tm, HP), lambda i: (i, 0)),
                 pl.BlockSpec((tm, MLA_CACHE_DIM), lambda i: (i, 0))]
    out_shape = [jax.ShapeDtypeStruct((M, HP), BF16),
                 jax.ShapeDtypeStruct((M, MLA_CACHE_DIM), F32)]
    if with_kv:
        out_specs += [pl.BlockSpec((tm, HP), lambda i: (i, 0)),
                      pl.BlockSpec((tm, HV), lambda i: (i, 0))]
        out_shape += [jax.ShapeDtypeStruct((M, HP), BF16),
                      jax.ShapeDtypeStruct((M, HV), BF16)]
    blocks = (_nbytes((tm, MLA_Q_RANK + KVG_W), F32) + _nbytes((MLA_Q_RANK, 2 * HP), BF16)
              + _nbytes((KIN, HP), BF16) + _nbytes((MLA_KV_RANK, HV), BF16)
              + 4 * _nbytes((tm, LANES), F32) + _nbytes((tm, 2 * HP + HV), BF16)
              + _nbytes((tm, 3 * LANES), F32) + _nbytes((tm, 2 * HP), F32))
    return pl.pallas_call(
        functools.partial(_mla_prep_kernel, with_kv=with_kv),
        grid=(M // tm,),
        in_specs=[pl.BlockSpec((tm, MLA_Q_RANK), lambda i: (i, Z_QA // MLA_Q_RANK)),
                  pl.BlockSpec((tm, KVG_W), lambda i: (i, Z_KVG // KVG_W)),
                  pl.BlockSpec((None, 1, MLA_Q_RANK), lambda i: (layer, 0, 0)),
                  pl.BlockSpec((None, 1, MLA_KV_RANK), lambda i: (layer, 0, 0)),
                  pl.BlockSpec((None, MLA_Q_RANK, 2 * HP), lambda i: (layer, 0, 0)),
                  pl.BlockSpec((None, KIN, HP), lambda i: (layer, 0, 0)),
                  pl.BlockSpec((None, MLA_KV_RANK, HV), lambda i: (layer, 0, 0)),
                  tab_spec, tab_spec, tab_spec, tab_spec],
        out_specs=out_specs,
        out_shape=out_shape,
        compiler_params=_params(blocks, 0, ("parallel",)),
        name="mla_prep" if with_kv else "mla_prep_decode",
    )(z, z, W["g_qa"], W["g_kva"], W["wq2"], W["wk"], W["wv"], *tabs)


def _mla_attn_kernel(q_ref, k_ref, v_ref, o_ref, *, tq, tk):
    qi = pl.program_id(2)
    nr = tq // tk
    c = (MLA_QK ** -0.5) * 1.4426950408889634
    vlane = lax.broadcasted_iota(jnp.int32, (tk, LANES), 1)
    olane = lax.broadcasted_iota(jnp.int32, (tk, LANES), 1)
    tri = lax.broadcasted_iota(jnp.int32, (tk, tk), 1) <= lax.broadcasted_iota(jnp.int32, (tk, tk), 0)

    def head(q, k, vt, ms, accs, first_row, masked_row):
        s_all = _dot_nt(q, k)
        new_ms, alphas, ps = [], [], []
        for i, r in enumerate(range(first_row, nr)):
            s = s_all[i * tk:(i + 1) * tk]
            if r == masked_row:
                s = jnp.where(tri, s, -jnp.inf)
            m_new = jnp.maximum(ms[i], jnp.max(s, axis=-1, keepdims=True))
            alphas.append(jnp.exp2((ms[i] - m_new) * c))
            ps.append(jnp.exp2((s - m_new) * c).astype(BF16))
            new_ms.append(m_new)
        pv = _dot(ps[0] if len(ps) == 1 else jnp.concatenate(ps, axis=0), vt)
        new_accs = [alphas[i] * accs[i] + pv[i * tk:(i + 1) * tk] for i in range(len(ps))]
        return new_ms, new_accs

    def block(kb, state, first_row, masked_row):
        ks = pl.multiple_of(kb * tk, tk)
        v = v_ref[pl.ds(ks, tk), :].astype(F32)
        v0 = jnp.where(vlane < MLA_V, v, jnp.where(vlane == MLA_V, 1.0, 0.0)).astype(BF16)
        v1 = jnp.where(vlane >= MLA_V, v, jnp.where(vlane == 0, 1.0, 0.0)).astype(BF16)
        k0 = k_ref[pl.ds(ks, tk), :HEAD_PAD]
        k1 = k_ref[pl.ds(ks, tk), HEAD_PAD:]
        live = state[first_row:]
        rows = slice(first_row * tk, tq)
        m0, a0 = head(q_ref[rows, :HEAD_PAD], k0, v0, [t[0] for t in live], [t[1] for t in live],
                      first_row, masked_row)
        m1, a1 = head(q_ref[rows, HEAD_PAD:], k1, v1, [t[2] for t in live], [t[3] for t in live],
                      first_row, masked_row)
        return state[:first_row] + tuple(zip(m0, a0, m1, a1))

    neg = jnp.full((tk, 1), -jnp.inf, F32)
    nil = jnp.zeros((tk, LANES), F32)
    state = tuple((neg, nil, neg, nil) for _ in range(nr))
    base = qi * nr
    state = lax.fori_loop(0, base, lambda kb, st: block(kb, st, 0, -1), state)
    for d in range(nr):
        state = block(base + d, state, d, d)
    for r in range(nr):
        m0, a0, m1, a1 = state[r]
        out = jnp.where(olane < MLA_V, a0 / a0[:, MLA_V:MLA_V + 1], a1 / a1[:, 0:1])
        o_ref[r * tk:(r + 1) * tk, :] = out.astype(o_ref.dtype)


def _mla_attn(q, k, v, B, T):
    tq = _pick(T, 512, 16)
    tk = _pick(T, 256, 16)
    assert tq % tk == 0
    nq = T // tq
    blocks = (_nbytes((tq, 2 * HEAD_PAD), BF16) + _nbytes((T, 2 * HEAD_PAD), BF16)
              + _nbytes((T, LANES), BF16) + _nbytes((tq, LANES), BF16))
    return pl.pallas_call(
        functools.partial(_mla_attn_kernel, tq=tq, tk=tk),
        grid=(B, MLA_HEADS // 2, nq),
        in_specs=[pl.BlockSpec((tq, 2 * HEAD_PAD), lambda b, hp, i: (b * nq + i, hp)),
                  pl.BlockSpec((T, 2 * HEAD_PAD), lambda b, hp, i: (b, hp)),
                  pl.BlockSpec((T, LANES), lambda b, hp, i: (b, hp))],
        out_specs=pl.BlockSpec((tq, LANES), lambda b, hp, i: (b * nq + i, hp)),
        out_shape=jax.ShapeDtypeStruct((B * T, MLA_HEADS * MLA_V), BF16),
        compiler_params=_params(blocks, 8 << 20, ("parallel", "parallel", "arbitrary")),
        name="mla_attn",
    )(q, k, v)


def _absorb_kernel(q_ref, w_ref, o_ref):
    o_ref[...] = _dot(q_ref[...], w_ref[...]).astype(o_ref.dtype)


def _absorb(q, wabs, layer):
    Bd = q.shape[0]
    QC = wabs.shape[-1]
    return pl.pallas_call(
        _absorb_kernel,
        grid=(MLA_HEADS,),
        in_specs=[pl.BlockSpec((Bd, HEAD_PAD), lambda h: (0, h)),
                  pl.BlockSpec((None, None, HEAD_PAD, QC), lambda h: (layer, h, 0, 0))],
        out_specs=pl.BlockSpec((Bd, QC), lambda h: (0, h)),
        out_shape=jax.ShapeDtypeStruct((Bd, MLA_HEADS * QC), BF16),
        name="mla_absorb",
    )(q, wabs)


def _page_copy(cache_ref, buf_ref, sem_ref, layer, page, slot, j, page_size):
    return pltpu.make_async_copy(cache_ref.at[layer, page],
                                 buf_ref.at[slot, :, pl.ds(j * page_size, page_size)],
                                 sem_ref.at[slot])


def _dec_attn_kernel(pt_ref, q_ref, new_ref, cache_ref, o_ref, buf_ref, kb_ref, sem_ref,
                     *, layer, n_pages, page_size, chunk):
    b = pl.program_id(0)
    nb = pl.num_programs(0)
    scale = MLA_QK ** -0.5
    R = MLA_KV_RANK

    def fetch(bb, slot):
        for j in range(n_pages):
            _page_copy(cache_ref, buf_ref, sem_ref, layer, pt_ref[bb * n_pages + j], slot, j,
                       page_size).start()

    @pl.when(b == 0)
    def _():
        fetch(0, 0)

    @pl.when(b + 1 < nb)
    def _():
        fetch(b + 1, (b + 1) % 2)

    slot = b % 2
    for j in range(n_pages):
        _page_copy(cache_ref, buf_ref, sem_ref, layer, 0, slot, j, page_size).wait()

    q = q_ref[0]
    q_lat = q[:, :R]
    q_rope = q[:, R:R + MLA_ROPE]
    n_chunks = (n_pages * page_size) // chunk
    parts = []
    for c in range(n_chunks):
        cols = slice(c * chunk, (c + 1) * chunk)
        rows = buf_ref[slot, :, cols].astype(BF16)
        kb_ref[:, cols] = rows[:R]
        parts.append(_dot(q_lat, rows[:R]) + _dot(q_rope, rows[R:]))
    s = jnp.concatenate(parts, axis=1)
    new = new_ref[0].astype(BF16).astype(F32)
    s_new = jnp.sum(q.astype(F32)[:, :MLA_CACHE_DIM] * new, axis=-1, keepdims=True)
    m = jnp.maximum(jnp.max(s, axis=-1, keepdims=True), s_new)
    p = jnp.exp((s - m) * scale)
    p_new = jnp.exp((s_new - m) * scale)
    l = jnp.sum(p, axis=-1, keepdims=True) + p_new
    acc = p_new * new[:, :R]
    pb = p.astype(BF16)
    for c in range(n_chunks):
        cols = slice(c * chunk, (c + 1) * chunk)
        acc = acc + _dot_nt(pb[:, cols], kb_ref[:, cols])
    o_ref[0] = acc / l


def _dec_attn(pt_flat, qcat, rows_new, cache_t, layer, n_pages):
    Bd = rows_new.shape[0]
    page_size = cache_t.shape[3]
    L = n_pages * page_size
    QC = qcat.shape[-1] // MLA_HEADS
    chunk = _pick(L, 1024, page_size)
    q3 = qcat.reshape(Bd, MLA_HEADS, QC)
    new3 = rows_new.reshape(Bd, 1, MLA_CACHE_DIM)
    scratch = 2 * _nbytes((MLA_CACHE_DIM, L), F32) + _nbytes((MLA_KV_RANK, L), BF16)
    grid_spec = pltpu.PrefetchScalarGridSpec(
        num_scalar_prefetch=1,
        grid=(Bd,),
        in_specs=[pl.BlockSpec((1, MLA_HEADS, QC), lambda b, pt: (b, 0, 0)),
                  pl.BlockSpec((1, 1, MLA_CACHE_DIM), lambda b, pt: (b, 0, 0)),
                  pl.BlockSpec(memory_space=pl.ANY)],
        out_specs=pl.BlockSpec((1, MLA_HEADS, MLA_KV_RANK), lambda b, pt: (b, 0, 0)),
        scratch_shapes=[pltpu.VMEM((2, MLA_CACHE_DIM, L), F32),
                        pltpu.VMEM((MLA_KV_RANK, L), BF16),
                        pltpu.SemaphoreType.DMA((2,))],
    )
    return pl.pallas_call(
        functools.partial(_dec_attn_kernel, layer=layer, n_pages=n_pages, page_size=page_size,
                          chunk=chunk),
        grid_spec=grid_spec,
        out_shape=jax.ShapeDtypeStruct((Bd, MLA_HEADS, MLA_KV_RANK), F32),
        compiler_params=_params(1 << 20, scratch, ("arbitrary",)),
        name="mla_decode_attn",
    )(pt_flat, q3, new3, cache_t)


def _dec_out_kernel(o_ref, w_ref, y_ref):
    y_ref[...] = _dot(o_ref[...].astype(BF16), w_ref[...]).astype(y_ref.dtype)


def _dec_out(o_lat2d, wuv_bd, layer):
    Bd = o_lat2d.shape[0]
    return pl.pallas_call(
        _dec_out_kernel,
        grid=(MLA_HEADS // 2,),
        in_specs=[pl.BlockSpec((Bd, 2 * MLA_KV_RANK), lambda hp: (0, hp)),
                  pl.BlockSpec((None, None, 2 * MLA_KV_RANK, 2 * MLA_V), lambda hp: (layer, hp, 0, 0))],
        out_specs=pl.BlockSpec((Bd, 2 * MLA_V), lambda hp: (0, hp)),
        out_shape=jax.ShapeDtypeStruct((Bd, MLA_HEADS * MLA_V), BF16),
        name="mla_decode_out",
    )(o_lat2d, wuv_bd)


def _log_decay(gg, wg, bg):
    g_pre = _dot(gg.astype(BF16), wg) + bg
    return (jnp.minimum(g_pre, 0.0) - jnp.log1p(jnp.exp(-jnp.abs(g_pre)))) / GLA_TAU


def _head_out(o, g_o, gr):
    return _rmsn(o, g_o) * (gr * jax.nn.sigmoid(gr))


def _gla_masks():
    C = GLA_CHUNK
    row = lax.broadcasted_iota(jnp.int32, (C, C), 0)
    col = lax.broadcasted_iota(jnp.int32, (C, C), 1)
    level = jnp.zeros((C, C), jnp.int32)
    for n, size in enumerate(GLA_LEVELS):
        sh = size.bit_length() - 1
        rb, cb = row >> sh, col >> sh
        level = jnp.where(rb - cb == 1, jnp.where((rb & 1) == 1, n + 1, level), level)
    rel = col - ((row >> 3) << 3)
    return level, rel, row >= col


def _gla_scores(q, k, b, level, rel, causal):
    C, DK = q.shape
    a = jnp.zeros((C, C), F32)
    for n, size in enumerate(GLA_LEVELS):
        ref = jnp.concatenate(
            [jnp.broadcast_to(b[p + size - 1:p + size, :], (2 * size, DK))
             for p in range(0, C, 2 * size)], axis=0)
        e = jnp.exp(-jnp.abs(b - ref))
        a = jnp.where(level == n + 1, _dot_nt((q * e).astype(BF16), (k * e).astype(BF16)), a)
    G = C // GLA_DIAG
    k3 = k.reshape(G, GLA_DIAG, DK)
    b3 = b.reshape(G, GLA_DIAG, DK)
    for j in range(GLA_DIAG):
        kj = jnp.broadcast_to(k3[:, j:j + 1, :], (G, GLA_DIAG, DK)).reshape(C, DK)
        bj = jnp.broadcast_to(b3[:, j:j + 1, :], (G, GLA_DIAG, DK)).reshape(C, DK)
        w = q * kj * jnp.exp(b - bj)
        a = jnp.where(rel == j, jnp.sum(w, axis=-1, keepdims=True), a)
    return jnp.where(causal, a, 0.0)


def _gla_prompt_kernel(gq_ref, gk_ref, gv_ref, gr_ref, gg_ref, wg_ref, bg_ref, go_ref,
                       y_ref, s_ref, la_ref):
    T = gq_ref.shape[0]
    C = GLA_CHUNK
    la_ref[...] = _log_decay(gg_ref[...], wg_ref[...], bg_ref[...])
    level, rel, causal = _gla_masks()
    tri = causal.astype(F32)
    g_o = go_ref[...]

    def chunk(c, st):
        r0 = pl.multiple_of(c * C, C)
        q = gq_ref[pl.ds(r0, C), :] * (GLA_DK ** -0.5)
        k = gk_ref[pl.ds(r0, C), :]
        v = gv_ref[pl.ds(r0, C), :].astype(BF16)
        b = jnp.dot(tri, la_ref[pl.ds(r0, C), :], precision=lax.Precision.HIGHEST,
                    preferred_element_type=F32)
        b_last = b[C - 1:C, :]
        a = _gla_scores(q, k, b, level, rel, causal)
        o = _dot_nt((q * jnp.exp(b)).astype(BF16), st.astype(BF16)) + _dot(a.astype(BF16), v)
        y_ref[pl.ds(r0, C), :] = _head_out(o, g_o, gr_ref[pl.ds(r0, C), :]).astype(y_ref.dtype)
        kd = (k * jnp.exp(b_last - b)).astype(BF16)
        return st * jnp.exp(b_last) + _dot_tn(v, kd)

    st = lax.fori_loop(0, T // C, chunk, jnp.zeros((GLA_DV, GLA_DK), F32), unroll=2)
    s_ref[...] = st.T


def _gla_prompt(z, W, layer, B, T):
    DK, DV, H = GLA_DK, GLA_DV, GLA_HEADS
    blocks = (3 * _nbytes((T, DK), F32) + 2 * _nbytes((T, DV), F32) + _nbytes((T, DV), BF16)
              + _nbytes((DK, DV), F32))
    scratch = _nbytes((T, DK), F32)
    return pl.pallas_call(
        _gla_prompt_kernel,
        grid=(B, H),
        in_specs=[pl.BlockSpec((T, DK), lambda b, h: (b, Z_GQ // DK + h)),
                  pl.BlockSpec((T, DK), lambda b, h: (b, Z_GK // DK + h)),
                  pl.BlockSpec((T, DV), lambda b, h: (b, Z_GV // DV + h)),
                  pl.BlockSpec((T, DV), lambda b, h: (b, Z_GR // DV + h)),
                  pl.BlockSpec((T, LANES), lambda b, h: (b, Z_GG // LANES)),
                  pl.BlockSpec((None, LANES, DK), lambda b, h: (layer, 0, h)),
                  pl.BlockSpec((None, 1, DK), lambda b, h: (layer, 0, h)),
                  pl.BlockSpec((None, 1, DV), lambda b, h: (layer, 0, 0))],
        out_specs=[pl.BlockSpec((T, DV), lambda b, h: (b, h)),
                   pl.BlockSpec((None, None, DK, DV), lambda b, h: (b, h, 0, 0))],
        out_shape=[jax.ShapeDtypeStruct((B * T, H * DV), BF16),
                   jax.ShapeDtypeStruct((B, H, DK, DV), F32)],
        scratch_shapes=[pltpu.VMEM((T, DK), F32)],
        compiler_params=_params(blocks, scratch, ("parallel", "parallel")),
        name="gla_prompt",
    )(z, z, z, z, z, W["wg2"], W["b_gla_g"], W["g_gla_o"])


def _gla_decode_kernel(gq_ref, gk_ref, gv_ref, gr_ref, gg_ref, s_ref, wg_ref, bg_ref, go_ref,
                       y_ref, ns_ref, o_scr):
    nb = gq_ref.shape[0]
    DK, DV, H = GLA_DK, GLA_DV, GLA_HEADS
    la = _log_decay(gg_ref[...], wg_ref[...], bg_ref[...])
    a = jnp.exp(la)
    q = gq_ref[...] * (GLA_DK ** -0.5)
    k = gk_ref[...]
    v = gv_ref[...]
    qa = q * a
    pieces = [x[:, h * DK:(h + 1) * DK] for x in (a, k, qa) for h in range(H)]
    pad = LANES - 3 * H * nb
    stack = jnp.concatenate(pieces + [jnp.zeros((pad, DK), F32)], axis=0)
    cols = stack.T
    qk = q * k
    for h in range(H):
        att = jnp.sum(qk[:, h * DK:(h + 1) * DK], axis=-1, keepdims=True)
        o_scr[:, h * DV:(h + 1) * DV] = att * v[:, h * DV:(h + 1) * DV]
    for i in range(nb):
        for h in range(H):
            c = h * nb + i
            a_col = cols[:, c:c + 1]
            k_col = cols[:, H * nb + c:H * nb + c + 1]
            qa_col = cols[:, 2 * H * nb + c:2 * H * nb + c + 1]
            s_old = s_ref[i, h]
            v_row = v[i:i + 1, h * DV:(h + 1) * DV]
            ns_ref[i, h] = a_col * s_old + k_col * v_row
            o_scr[i:i + 1, h * DV:(h + 1) * DV] += jnp.sum(qa_col * s_old, axis=0, keepdims=True)
    g_o = go_ref[...]
    gr = gr_ref[...]
    for h in range(H):
        sl = slice(h * DV, (h + 1) * DV)
        y_ref[:, sl] = _head_out(o_scr[:, sl], g_o, gr[:, sl]).astype(y_ref.dtype)


def _gla_decode(z, state, W, layer):
    Bd = z.shape[0]
    DK, DV, H = GLA_DK, GLA_DV, GLA_HEADS
    nb = 8
    HK, HV = H * DK, H * DV
    blocks = (2 * _nbytes((nb, HK), F32) + 3 * _nbytes((nb, HV), F32) + _nbytes((nb, LANES), F32)
              + 2 * _nbytes((nb, H, DK, DV), F32) + _nbytes((LANES, HK), BF16))
    return pl.pallas_call(
        _gla_decode_kernel,
        grid=(Bd // nb,),
        in_specs=[pl.BlockSpec((nb, HK), lambda i: (i, Z_GQ // HK)),
                  pl.BlockSpec((nb, HK), lambda i: (i, Z_GK // HK)),
                  pl.BlockSpec((nb, HV), lambda i: (i, Z_GV // HV)),
                  pl.BlockSpec((nb, HV), lambda i: (i, Z_GR // HV)),
                  pl.BlockSpec((nb, LANES), lambda i: (i, Z_GG // LANES)),
                  pl.BlockSpec((None, nb, H, DK, DV), lambda i: (layer, i, 0, 0, 0)),
                  pl.BlockSpec((None, LANES, HK), lambda i: (layer, 0, 0)),
                  pl.BlockSpec((None, 1, HK), lambda i: (layer, 0, 0)),
                  pl.BlockSpec((None, 1, DV), lambda i: (layer, 0, 0))],
        out_specs=[pl.BlockSpec((nb, HV), lambda i: (i, 0)),
                   pl.BlockSpec((nb, H, DK, DV), lambda i: (i, 0, 0, 0))],
        out_shape=[jax.ShapeDtypeStruct((Bd, HV), F32),
                   jax.ShapeDtypeStruct((Bd, H, DK, DV), F32)],
        scratch_shapes=[pltpu.VMEM((nb, HV), F32)],
        compiler_params=_params(blocks, 0, ("parallel",)),
        name="gla_decode",
    )(z, z, z, z, z, state, W["wg2"], W["b_gla_g"], W["g_gla_o"])


def _merge_kernel(yc_ref, ym_ref, yg_ref, gc_ref, gm_ref, gg_ref, wc_ref, wm_ref, wg_ref, o_ref):
    tm = o_ref.shape[0]
    rs = min(ROW_SUB, tm)
    for r in range(0, tm, rs):
        rows = slice(r, r + rs)
        acc = jax.nn.sigmoid(gc_ref[rows, :]) * _dot(yc_ref[rows, :].astype(BF16), wc_ref[...])
        acc = acc + jax.nn.sigmoid(gm_ref[rows, :]) * _dot(ym_ref[rows, :].astype(BF16), wm_ref[...])
        acc = acc + jax.nn.sigmoid(gg_ref[rows, :]) * _dot(yg_ref[rows, :].astype(BF16), wg_ref[...])
        o_ref[rows, :] = acc.astype(o_ref.dtype)


def _merge(y_conv, y_mla, y_gla, z, W, layer):
    M = z.shape[0]
    tm = _pick(M, 1024, 16)
    tn = 512
    C = y_conv.shape[1]
    ga = Z_GA // tn
    gstep = D_MODEL // tn
    blocks = (3 * _nbytes((tm, C), F32) + 3 * _nbytes((tm, tn), F32) + 3 * _nbytes((C, tn), BF16)
              + _nbytes((tm, tn), BF16))
    y_spec = pl.BlockSpec((tm, C), lambda i, j: (i, 0))
    w_spec = pl.BlockSpec((None, C, tn), lambda i, j: (layer, 0, j))
    return pl.pallas_call(
        _merge_kernel,
        grid=(M // tm, D_MODEL // tn),
        in_specs=[y_spec, y_spec, y_spec,
                  pl.BlockSpec((tm, tn), lambda i, j: (i, ga + j)),
                  pl.BlockSpec((tm, tn), lambda i, j: (i, ga + gstep + j)),
                  pl.BlockSpec((tm, tn), lambda i, j: (i, ga + 2 * gstep + j)),
                  w_spec, w_spec, w_spec],
        out_specs=pl.BlockSpec((tm, tn), lambda i, j: (i, j)),
        out_shape=jax.ShapeDtypeStruct((M, D_MODEL), BF16),
        compiler_params=_params(blocks, 0, ("parallel", "arbitrary")),
        name="merge",
    )(y_conv, y_mla, y_gla, z, z, z, W["w_br_conv"], W["w_br_mla"], W["w_br_gla"])


def _ffn_up_kernel(h_ref, halo_ref, g_ref, wg_ref, wv_ref, cg_ref, cv_ref,
                   act_ref, sg_ref, sv_ref, xn_ref, *, tiles_per_seq, row_tiles):
    tm = h_ref.shape[0]
    i = pl.program_id(0)

    @pl.when(pl.program_id(1) == 0)
    def _():
        g = g_ref[...]
        halo = _rmsn(halo_ref[...], g)
        halo = jnp.where(i % tiles_per_seq == 0, 0.0, halo)
        xn_ref[:HALO, :] = halo.astype(BF16)
        xn_ref[HALO:, :] = _rmsn(h_ref[...], g).astype(BF16)

    def conv(u, c_ref):
        out = pltpu.roll(u, 2, axis=0) * c_ref[0:1, :]
        out = out + pltpu.roll(u, 1, axis=0) * c_ref[1:2, :]
        out = out + u * c_ref[2:3, :]
        return out[HALO:]

    rs = tm // row_tiles
    for r in range(row_tiles):
        xn = xn_ref[r * rs:(r + 1) * rs + HALO, :]
        ug = _dot(xn, wg_ref[...])
        uv = _dot(xn, wv_ref[...])
        gate = conv(ug, cg_ref)
        val = conv(uv, cv_ref)
        act_ref[r * rs:(r + 1) * rs, :] = (gate * jax.nn.sigmoid(gate) * val).astype(act_ref.dtype)
        if r == row_tiles - 1:
            sg_ref[...] = ug[HALO + rs - 2:, :]
            sv_ref[...] = uv[HALO + rs - 2:, :]


def _ffn_up_prompt(h, W, layer, B, T):
    M, K = h.shape
    tm = _pick(T, 1024, HALO)
    tn = _pick(D_FF, 512, LANES)
    nt = T // tm
    nj = D_FF // tn
    blocks = (_nbytes((tm + HALO, K), F32) + 2 * _nbytes((K, tn), BF16) + _nbytes((tm, tn), BF16)
              + 4 * _nbytes((tm + HALO, tn), F32))
    scratch = _nbytes((tm + HALO, K), BF16)
    hb = tm // HALO
    act, sg, sv = pl.pallas_call(
        functools.partial(_ffn_up_kernel, tiles_per_seq=nt, row_tiles=max(tm // ROW_SUB, 1)),
        grid=(M // tm, nj),
        in_specs=[pl.BlockSpec((tm, K), lambda i, j: (i, 0)),
                  pl.BlockSpec((HALO, K), lambda i, j: (jnp.maximum(i * hb - 1, 0), 0)),
                  pl.BlockSpec((None, 1, K), lambda i, j: (layer, 0, 0)),
                  pl.BlockSpec((None, K, tn), lambda i, j: (layer, 0, j)),
                  pl.BlockSpec((None, K, tn), lambda i, j: (layer, 0, nj + j)),
                  pl.BlockSpec((None, 3, tn), lambda i, j: (layer, 0, j)),
                  pl.BlockSpec((None, 3, tn), lambda i, j: (layer, 0, nj + j))],
        out_specs=[pl.BlockSpec((tm, tn), lambda i, j: (i, j)),
                   pl.BlockSpec((None, 2, tn), lambda i, j: (i, 0, j)),
                   pl.BlockSpec((None, 2, tn), lambda i, j: (i, 0, j))],
        out_shape=[jax.ShapeDtypeStruct((M, D_FF), BF16),
                   jax.ShapeDtypeStruct((M // tm, 2, D_FF), F32),
                   jax.ShapeDtypeStruct((M // tm, 2, D_FF), F32)],
        scratch_shapes=[pltpu.VMEM((tm + HALO, K), BF16)],
        compiler_params=_params(blocks, scratch, ("parallel", "arbitrary")),
        name="ffn_up",
    )(h, h, W["g_ffn"], W["w_up"], W["w_up"], W["w_ffn_conv"], W["w_ffn_conv"])
    return act, sg[nt - 1::nt], sv[nt - 1::nt]


def _ffn_conv_decode_kernel(ug_ref, uv_ref, s0g_ref, s0v_ref, s1g_ref, s1v_ref, cg_ref, cv_ref,
                            act_ref):
    gate = s0g_ref[...] * cg_ref[0:1, :] + s1g_ref[...] * cg_ref[1:2, :] + ug_ref[...] * cg_ref[2:3, :]
    val = s0v_ref[...] * cv_ref[0:1, :] + s1v_ref[...] * cv_ref[1:2, :] + uv_ref[...] * cv_ref[2:3, :]
    act_ref[...] = (gate * jax.nn.sigmoid(gate) * val).astype(act_ref.dtype)


def _ffn_conv_decode(u, state2d, w_ffn_conv, layer):
    Bd = u.shape[0]
    tn = _pick(D_FF, 1024, LANES)
    nj = D_FF // tn
    blocks = 6 * _nbytes((Bd, tn), F32) + _nbytes((Bd, tn), BF16)
    u_spec = lambda off: pl.BlockSpec((Bd, tn), lambda j: (0, off + j))
    s_spec = lambda off: pl.BlockSpec((None, Bd, tn), lambda j: (layer, 0, off + j))
    c_spec = lambda off: pl.BlockSpec((None, 3, tn), lambda j: (layer, 0, off + j))
    return pl.pallas_call(
        _ffn_conv_decode_kernel,
        grid=(nj,),
        in_specs=[u_spec(0), u_spec(nj), s_spec(0), s_spec(nj), s_spec(2 * nj), s_spec(3 * nj),
                  c_spec(0), c_spec(nj)],
        out_specs=pl.BlockSpec((Bd, tn), lambda j: (0, j)),
        out_shape=jax.ShapeDtypeStruct((Bd, D_FF), BF16),
        compiler_params=_params(blocks, 0, ("parallel",)),
        name="ffn_conv_decode",
    )(u, u, state2d, state2d, state2d, state2d, w_ffn_conv, w_ffn_conv)


def _ple_kernel(h_ref, g_ref, wg_ref, p_ref, wp_ref, o_ref, xn_ref):
    tn = o_ref.shape[1]
    j = pl.program_id(1)

    @pl.when(j == 0)
    def _():
        xn_ref[...] = _rmsn(h_ref[...], g_ref[...]).astype(BF16)

    tm = o_ref.shape[0]
    rs = min(ROW_SUB, tm)
    for r in range(0, tm, rs):
        rows = slice(r, r + rs)
        gate = jax.nn.sigmoid(_dot(xn_ref[rows, :], wg_ref[...]))
        emb = _dot(p_ref[rows, :].astype(BF16), wp_ref[...])
        res = h_ref[rows, pl.ds(pl.multiple_of(j * tn, tn), tn)]
        o_ref[rows, :] = res + gate * emb


def _ple(h, p, W, layer):
    M, K = h.shape
    tm = _pick(M, 1024, 16)
    tn = 512
    blocks = (_nbytes((tm, K), F32) + _nbytes((K, tn), BF16) + _nbytes((tm, PLE_DIM), F32)
              + _nbytes((PLE_DIM, tn), BF16) + _nbytes((tm, tn), F32))
    return pl.pallas_call(
        _ple_kernel,
        grid=(M // tm, K // tn),
        in_specs=[pl.BlockSpec((tm, K), lambda i, j: (i, 0)),
                  pl.BlockSpec((None, 1, K), lambda i, j: (layer, 0, 0)),
                  pl.BlockSpec((None, K, tn), lambda i, j: (layer, 0, j)),
                  pl.BlockSpec((None, tm, PLE_DIM), lambda i, j: (layer, i, 0)),
                  pl.BlockSpec((None, PLE_DIM, tn), lambda i, j: (layer, 0, j))],
        out_specs=pl.BlockSpec((tm, tn), lambda i, j: (i, j)),
        out_shape=jax.ShapeDtypeStruct((M, K), F32),
        scratch_shapes=[pltpu.VMEM((tm, K), BF16)],
        compiler_params=_params(blocks, _nbytes((tm, K), BF16), ("parallel", "arbitrary")),
        name="ple",
    )(h, W["g_ple"], W["w_ple_gate"], p, W["w_ple"])


def _final_norm_kernel(h_ref, g_ref, o_ref):
    o_ref[...] = _rmsn(h_ref[...], g_ref[...])


def _final_norm(h, g):
    M, K = h.shape
    tm = _pick(M, 512)
    return pl.pallas_call(
        _final_norm_kernel,
        grid=(M // tm,),
        in_specs=[pl.BlockSpec((tm, K), lambda i: (i, 0)),
                  pl.BlockSpec((1, K), lambda i: (0, 0))],
        out_specs=pl.BlockSpec((tm, K), lambda i: (i, 0)),
        out_shape=jax.ShapeDtypeStruct((M, K), F32),
        compiler_params=_params(2 * _nbytes((tm, K), F32), 0, ("parallel",)),
        name="final_norm",
    )(h, g)


def _w_in_segments():
    offs = [0]
    for s in IN_SPLITS:
        offs.append(offs[-1] + s)
    dst = {0: Z_CB, 1: Z_CC, 2: Z_CH, 10: Z_GA, 11: Z_GA + D_MODEL, 12: Z_GA + 2 * D_MODEL,
           7: Z_GV, 9: Z_GR, 3: Z_QA, 5: Z_GQ, 6: Z_GK, 4: Z_KVG, 8: Z_GG}
    return [(offs[n], IN_SPLITS[n], d) for n, d in dst.items()]


def _w_in_relayout_kernel(w_ref, o_ref):
    for src, width, dst in _w_in_segments():
        o_ref[:, dst:dst + width] = w_ref[:, src:src + width].astype(o_ref.dtype)
    for start, stop in ((Z_KVG + MLA_CACHE_DIM, Z_GG), (Z_GG + GLA_GATE_RANK, Z_DIM)):
        o_ref[:, start:stop] = jnp.zeros((o_ref.shape[0], stop - start), o_ref.dtype)


def _w_in_relayout(w_in):
    depth, K, N = w_in.shape
    tr = LANES
    blocks = _nbytes((tr, N), F32) + _nbytes((tr, Z_DIM), BF16)
    return pl.pallas_call(
        _w_in_relayout_kernel,
        grid=(depth, K // tr),
        in_specs=[pl.BlockSpec((None, tr, N), lambda d, r: (d, r, 0))],
        out_specs=pl.BlockSpec((None, tr, Z_DIM), lambda d, r: (d, r, 0)),
        out_shape=jax.ShapeDtypeStruct((depth, K, Z_DIM), BF16),
        compiler_params=_params(blocks, 0, ("parallel", "parallel")),
        name="w_in_relayout",
    )(w_in)


def _prep_weights(w):
    depth = w["w_in"].shape[0]
    H, R = MLA_HEADS, MLA_KV_RANK
    w_in = _w_in_relayout(w["w_in"])

    wq = w["w_qb"].reshape(depth, MLA_Q_RANK, H, MLA_QK)
    half = MLA_ROPE // 2
    x1 = wq[..., MLA_NOPE:MLA_NOPE + half]
    x2 = wq[..., MLA_NOPE + half:]
    pad_q = jnp.zeros((depth, MLA_Q_RANK, H, HEAD_PAD - MLA_QK), F32)
    wq_p = jnp.concatenate([wq, pad_q], axis=-1)
    wq_sw = jnp.concatenate([jnp.zeros_like(wq[..., :MLA_NOPE]), -x2, x1, pad_q], axis=-1)
    wq2 = jnp.concatenate([wq_p.reshape(depth, MLA_Q_RANK, H * HEAD_PAD),
                           wq_sw.reshape(depth, MLA_Q_RANK, H * HEAD_PAD)], axis=-1).astype(BF16)

    wkv = w["w_kvb"].reshape(depth, R, H, MLA_NOPE + MLA_V)
    w_uk, w_uv = wkv[..., :MLA_NOPE], wkv[..., MLA_NOPE:]
    wk_top = jnp.concatenate([w_uk, jnp.zeros((depth, R, H, HEAD_PAD - MLA_NOPE), F32)], axis=-1)
    eye = jnp.eye(MLA_ROPE, dtype=F32)
    copy = jnp.concatenate([jnp.zeros((MLA_ROPE, MLA_NOPE), F32), eye,
                            jnp.zeros((MLA_ROPE, HEAD_PAD - MLA_QK), F32)], axis=-1)
    copy = jnp.broadcast_to(copy[:, None, :], (MLA_ROPE, H, HEAD_PAD))
    wk_bot = jnp.concatenate([copy, jnp.zeros((LANES - MLA_ROPE, H, HEAD_PAD), F32)], axis=0)
    wk = jnp.concatenate([wk_top, jnp.broadcast_to(wk_bot[None], (depth, LANES, H, HEAD_PAD))],
                         axis=1).reshape(depth, R + LANES, H * HEAD_PAD).astype(BF16)
    wv = w_uv.reshape(depth, R, H * MLA_V).astype(BF16)

    QC = R + LANES
    uk_t = jnp.transpose(w_uk, (0, 2, 3, 1))
    top = jnp.concatenate([uk_t, jnp.zeros((depth, H, MLA_NOPE, LANES), F32)], axis=-1)
    mid = jnp.concatenate([jnp.zeros((MLA_ROPE, R), F32), eye,
                           jnp.zeros((MLA_ROPE, LANES - MLA_ROPE), F32)], axis=-1)
    mid = jnp.broadcast_to(mid[None, None], (depth, H, MLA_ROPE, QC))
    bot = jnp.zeros((depth, H, HEAD_PAD - MLA_QK, QC), F32)
    wabs = jnp.concatenate([top, mid, bot], axis=2).astype(BF16)
    uv_h = jnp.transpose(w_uv, (0, 2, 1, 3)).reshape(depth, H // 2, 2, R, MLA_V)
    zed = jnp.zeros_like(uv_h[:, :, 0])
    wuv_bd = jnp.concatenate([jnp.concatenate([uv_h[:, :, 0], zed], axis=-1),
                              jnp.concatenate([zed, uv_h[:, :, 1]], axis=-1)], axis=2).astype(BF16)

    wg2 = jnp.concatenate([w["w_gla_g2"],
                           jnp.zeros((depth, LANES - GLA_GATE_RANK, GLA_HEADS * GLA_DK), F32)],
                          axis=1).astype(BF16)
    row = lambda a: a[:, None, :]
    out = dict(w_in=w_in, wq2=wq2, wk=wk, wv=wv, wabs=wabs, wuv_bd=wuv_bd, wg2=wg2,
               g_mix=row(w["g_mix"]), g_qa=row(w["g_qa"]), g_kva=row(w["g_kva"]),
               b_gla_g=row(w["b_gla_g"]), g_gla_o=row(w["g_gla_o"]), g_ffn=row(w["g_ffn"]),
               g_ple=row(w["g_ple"]), w_conv=w["w_conv"], w_ffn_conv=w["w_ffn_conv"])
    for name in ("w_br_conv", "w_br_mla", "w_br_gla", "w_o", "w_up", "w_down", "w_ple_gate", "w_ple"):
        out[name] = w[name].astype(BF16)
    return out


def _rope_tables(pos):
    half = MLA_ROPE // 2
    inv = ROPE_THETA ** (-jnp.arange(half, dtype=F32) / half)
    ang = pos.astype(F32)[:, None] * inv[None, :]
    cos, sin = jnp.cos(ang), jnp.sin(ang)
    n = pos.shape[0]
    ones = jnp.ones((n, MLA_NOPE), F32)
    zq = jnp.zeros((n, HEAD_PAD - MLA_QK), F32)
    zk = jnp.zeros((n, LANES - MLA_ROPE), F32)
    c_q = jnp.concatenate([ones, cos, cos, zq], axis=1)
    s_q = jnp.concatenate([jnp.zeros_like(ones), sin, sin, zq], axis=1)
    c_k = jnp.concatenate([cos, cos, zk], axis=1)
    s_k = jnp.concatenate([-sin, sin, zk], axis=1)
    return c_q, s_q, c_k, s_k


def _layer_tail(h, z, y_conv, y_mla, y_gla, W, layer):
    merged = _merge(y_conv, y_mla, y_gla, z, W, layer)
    return _matmul_res(merged, W["w_o"], layer, h, tm_pref=512, tn_pref=2048, name="out_proj")


def _prompt_trunk(x, p, W, g_final):
    B, T, D = x.shape
    depth = W["w_in"].shape[0]
    M = B * T
    h = x.reshape(M, D)
    p2 = p.reshape(depth, M, PLE_DIM)
    tabs = _rope_tables(jnp.arange(T, dtype=jnp.int32))
    rows, conv_st, gla_st, ffn_st = [], [], [], []
    for i in range(depth):
        z = _rms_matmul(h, W["g_mix"], W["w_in"], i, tm_pref=1024, tn_pref=1024, name="in_proj")
        y_conv, cst = _conv_prompt(z, W["w_conv"], i, B, T)
        q, mla_rows, k, v = _mla_prep(z, W, i, tabs, T, with_kv=True)
        y_mla = _mla_attn(q, k, v, B, T)
        y_gla, s_new = _gla_prompt(z, W, i, B, T)
        h = _layer_tail(h, z, y_conv, y_mla, y_gla, W, i)
        act, sg, sv = _ffn_up_prompt(h, W, i, B, T)
        h = _matmul_res(act, W["w_down"], i, h, tm_pref=512, tn_pref=1024, name="ffn_down")
        h = _ple(h, p2, W, i)
        rows.append(mla_rows.reshape(B, T, MLA_CACHE_DIM))
        conv_st.append(cst)
        gla_st.append(s_new)
        ffn_st.append(jnp.concatenate([sg, sv], axis=-1))
    y = _final_norm(h, g_final).reshape(B, T, D)
    return y, jnp.stack(rows), jnp.stack(conv_st), jnp.stack(gla_st), jnp.stack(ffn_st)


def _decode_trunk(x, p, W, g_final, cache, page_table, st_conv, st_gla, st_ffn):
    Bd, T, D = x.shape
    assert T == 1, "decode path handles one new token per request"
    depth = W["w_in"].shape[0]
    n_pages = page_table.shape[1]
    past_len = n_pages * cache.shape[2]
    h = x.reshape(Bd, D)
    p2 = p.reshape(depth, Bd, PLE_DIM)
    tabs = _rope_tables(jnp.full((Bd,), past_len, dtype=jnp.int32))
    pt_flat = page_table.reshape(-1)
    cache_t = jnp.swapaxes(cache, 2, 3)
    st_conv2 = st_conv.reshape(depth, Bd, 2 * CONV_DIM)
    st_ffn2 = st_ffn.reshape(depth, Bd, 4 * D_FF)
    rows, conv_st, gla_st, ffn_st = [], [], [], []
    for i in range(depth):
        z = _rms_matmul(h, W["g_mix"], W["w_in"], i, tm_pref=1024, tn_pref=1024, name="in_proj_decode")
        y_conv, cst = _conv_decode(z, st_conv2, W["w_conv"], i)
        q, mla_rows = _mla_prep(z, W, i, tabs, Bd, with_kv=False)
        qcat = _absorb(q, W["wabs"], i)
        o_lat = _dec_attn(pt_flat, qcat, mla_rows, cache_t, i, n_pages)
        y_mla = _dec_out(o_lat.reshape(Bd, MLA_HEADS * MLA_KV_RANK), W["wuv_bd"], i)
        y_gla, s_new = _gla_decode(z, st_gla, W, i)
        h = _layer_tail(h, z, y_conv, y_mla, y_gla, W, i)
        u = _rms_matmul(h, W["g_ffn"], W["w_up"], i, tm_pref=1024, tn_pref=1024, name="ffn_up_decode")
        act = _ffn_conv_decode(u, st_ffn2, W["w_ffn_conv"], i)
        h = _matmul_res(act, W["w_down"], i, h, tm_pref=512, tn_pref=1024, name="ffn_down_decode")
        h = _ple(h, p2, W, i)
        rows.append(mla_rows.reshape(Bd, 1, MLA_CACHE_DIM))
        conv_st.append(cst.reshape(Bd, 2, CONV_DIM))
        gla_st.append(s_new)
        ffn_st.append(jnp.stack([st_ffn[i, :, 1, :], u], axis=1))
    y = _final_norm(h, g_final).reshape(Bd, 1, D)
    return y, jnp.stack(rows), jnp.stack(conv_st), jnp.stack(gla_st), jnp.stack(ffn_st)


def kernel(x_prompt, x_sample, cache_mla, state_conv, state_gla, state_ffn, page_table, p_prompt, p_sample, g_mix, w_in, w_conv, g_qa, w_qb, g_kva, w_kvb, w_gla_g2, b_gla_g, g_gla_o, w_br_conv, w_br_mla, w_br_gla, w_o, g_ffn, w_up, w_ffn_conv, w_down, g_ple, w_ple_gate, w_ple, g_final):
    W = _prep_weights(dict(
        g_mix=g_mix, w_in=w_in, w_conv=w_conv, g_qa=g_qa, w_qb=w_qb, g_kva=g_kva, w_kvb=w_kvb,
        w_gla_g2=w_gla_g2, b_gla_g=b_gla_g, g_gla_o=g_gla_o, w_br_conv=w_br_conv,
        w_br_mla=w_br_mla, w_br_gla=w_br_gla, w_o=w_o, g_ffn=g_ffn, w_up=w_up,
        w_ffn_conv=w_ffn_conv, w_down=w_down, g_ple=g_ple, w_ple_gate=w_ple_gate, w_ple=w_ple))
    g_fin = g_final[None, :]
    y_p, mla_p, conv_p, gla_p, ffn_p = _prompt_trunk(x_prompt, p_prompt, W, g_fin)
    y_s, mla_s, conv_s, gla_s, ffn_s = _decode_trunk(
        x_sample, p_sample, W, g_fin, cache_mla, page_table, state_conv, state_gla, state_ffn)
    return (y_p, y_s, mla_p, mla_s, conv_p, conv_s, gla_p, gla_s, ffn_p, ffn_s)
```

```python
import functools

import jax
import jax.numpy as jnp
from jax import lax
from jax.experimental import pallas as pl
from jax.experimental.pallas import tpu as pltpu

F32 = jnp.float32
BF16 = jnp.bfloat16

D_MODEL = 2048
CONV_DIM = D_MODEL // 2
MLA_HEADS = 16
MLA_NOPE = 64
MLA_ROPE = 32
MLA_QK = MLA_NOPE + MLA_ROPE
MLA_V = 64
MLA_Q_RANK = D_MODEL // 4
MLA_KV_RANK = D_MODEL // 8
MLA_CACHE_DIM = MLA_KV_RANK + MLA_ROPE
ROPE_THETA = 10000.0
GLA_HEADS = 4
GLA_DK = D_MODEL // 16
GLA_DV = D_MODEL // 8
GLA_GATE_RANK = 16
GLA_TAU = 16.0
GLA_CHUNK = 64
GLA_LEVELS = (32, 16, 8)
GLA_DIAG = 8
D_FF = ((8 * D_MODEL // 3 + 255) // 256) * 256
PLE_DIM = 256
EPS = 1e-6

IN_SPLITS = (CONV_DIM, CONV_DIM, CONV_DIM, MLA_Q_RANK, MLA_CACHE_DIM,
             GLA_HEADS * GLA_DK, GLA_HEADS * GLA_DK, GLA_HEADS * GLA_DV, GLA_GATE_RANK,
             GLA_HEADS * GLA_DV, D_MODEL, D_MODEL, D_MODEL)

LANES = 128
HEAD_PAD = LANES
VMEM_CAP = 60 * 1024 * 1024
HALO = 16
ROW_SUB = 256
COL_SUB = 256

Z_CB, Z_CC, Z_CH = 0, CONV_DIM, 2 * CONV_DIM
Z_GA = 3 * CONV_DIM
Z_GV = Z_GA + 3 * D_MODEL
Z_GR = Z_GV + GLA_HEADS * GLA_DV
Z_QA = Z_GR + GLA_HEADS * GLA_DV
Z_GQ = Z_QA + MLA_Q_RANK
Z_GK = Z_GQ + GLA_HEADS * GLA_DK
Z_KVG = Z_GK + GLA_HEADS * GLA_DK
KVG_W = 4 * LANES
Z_GG = Z_KVG + 3 * LANES
Z_DIM = Z_KVG + KVG_W


def _pick(n, pref, mult=8):
    if n <= pref:
        return n
    for t in range(pref, 0, -1):
        if n % t == 0 and t % mult == 0:
            return t
    raise ValueError(f"no tile for {n}")


def _params(block_bytes, scratch_bytes=0, sem=None):
    need = 2 * block_bytes + scratch_bytes + (8 << 20)
    kw = dict(vmem_limit_bytes=int(min(max(need, 32 << 20), VMEM_CAP)))
    if sem is not None:
        kw["dimension_semantics"] = sem
    return pltpu.CompilerParams(**kw)


def _nbytes(shape, dtype):
    n = 1
    for s in shape:
        n *= s
    return n * jnp.dtype(dtype).itemsize


def _rmsn(x, g):
    y = x * lax.rsqrt(jnp.mean(x * x, axis=-1, keepdims=True) + EPS)
    return y * g


def _dot(a, b):
    return jnp.dot(a, b, preferred_element_type=F32)


def _dot_nt(a, b):
    return lax.dot_general(a, b, (((1,), (1,)), ((), ())), preferred_element_type=F32)


def _dot_tn(a, b):
    return lax.dot_general(a, b, (((0,), (0,)), ((), ())), preferred_element_type=F32)


def _rms_mm_kernel(x_ref, g_ref, w_ref, o_ref, xn_ref):
    @pl.when(pl.program_id(1) == 0)
    def _():
        xn_ref[...] = _rmsn(x_ref[...], g_ref[...]).astype(BF16)

    o_ref[...] = _dot(xn_ref[...], w_ref[...])


def _rms_matmul(x, g, w, layer, *, tm_pref, tn_pref, name):
    M, K = x.shape
    N = w.shape[-1]
    tm = _pick(M, tm_pref)
    tn = _pick(N, tn_pref, LANES)
    blocks = _nbytes((tm, K), F32) + _nbytes((K, tn), BF16) + _nbytes((tm, tn), F32)
    return pl.pallas_call(
        _rms_mm_kernel,
        grid=(M // tm, N // tn),
        in_specs=[pl.BlockSpec((tm, K), lambda i, j: (i, 0)),
                  pl.BlockSpec((None, 1, K), lambda i, j: (layer, 0, 0)),
                  pl.BlockSpec((None, K, tn), lambda i, j: (layer, 0, j))],
        out_specs=pl.BlockSpec((tm, tn), lambda i, j: (i, j)),
        out_shape=jax.ShapeDtypeStruct((M, N), F32),
        scratch_shapes=[pltpu.VMEM((tm, K), BF16)],
        compiler_params=_params(blocks, _nbytes((tm, K), BF16), ("parallel", "arbitrary")),
        name=name,
    )(x, g, w)


def _mm_res_kernel(x_ref, w_ref, r_ref, o_ref):
    o_ref[...] = r_ref[...] + _dot(x_ref[...].astype(BF16), w_ref[...])


def _matmul_res(x, w, layer, res, *, tm_pref, tn_pref, name):
    M, K = x.shape
    N = w.shape[-1]
    tm = _pick(M, tm_pref, 16)
    tn = _pick(N, tn_pref, LANES)
    blocks = _nbytes((tm, K), x.dtype) + _nbytes((K, tn), BF16) + 2 * _nbytes((tm, tn), F32)
    return pl.pallas_call(
        _mm_res_kernel,
        grid=(N // tn, M // tm),
        in_specs=[pl.BlockSpec((tm, K), lambda j, i: (i, 0)),
                  pl.BlockSpec((None, K, tn), lambda j, i: (layer, 0, j)),
                  pl.BlockSpec((tm, tn), lambda j, i: (i, j))],
        out_specs=pl.BlockSpec((tm, tn), lambda j, i: (i, j)),
        out_shape=jax.ShapeDtypeStruct((M, N), F32),
        compiler_params=_params(blocks, 0, ("parallel", "parallel")),
        name=name,
    )(x, w, res)


def _shift_rows(x, k):
    rows = lax.broadcasted_iota(jnp.int32, x.shape, 0)
    return jnp.where(rows >= k, pltpu.roll(x, k, axis=0), 0.0)


def _conv_prompt_kernel(cb_ref, cc_ref, ch_ref, w_ref, y_ref, st_ref):
    T = cb_ref.shape[0]
    x = cc_ref[...] * ch_ref[...]
    zc = _shift_rows(x, 2) * w_ref[0:1, :] + _shift_rows(x, 1) * w_ref[1:2, :] + x * w_ref[2:3, :]
    y_ref[...] = (cb_ref[...] * zc).astype(y_ref.dtype)
    st_ref[...] = x[T - 2:T, :]


def _conv_prompt(z, w_conv, layer, B, T):
    tc = 2 * LANES
    nc = CONV_DIM // tc
    blocks = 3 * _nbytes((T, tc), F32) + _nbytes((T, tc), BF16)
    return pl.pallas_call(
        _conv_prompt_kernel,
        grid=(B, nc),
        in_specs=[pl.BlockSpec((T, tc), lambda b, c: (b, Z_CB // tc + c)),
                  pl.BlockSpec((T, tc), lambda b, c: (b, Z_CC // tc + c)),
                  pl.BlockSpec((T, tc), lambda b, c: (b, Z_CH // tc + c)),
                  pl.BlockSpec((None, 3, tc), lambda b, c: (layer, 0, c))],
        out_specs=[pl.BlockSpec((T, tc), lambda b, c: (b, c)),
                   pl.BlockSpec((None, 2, tc), lambda b, c: (b, 0, c))],
        out_shape=[jax.ShapeDtypeStruct((B * T, CONV_DIM), BF16),
                   jax.ShapeDtypeStruct((B, 2, CONV_DIM), F32)],
        compiler_params=_params(blocks, 0, ("parallel", "parallel")),
        name="conv_prompt",
    )(z, z, z, w_conv)


def _conv_decode_kernel(cb_ref, cc_ref, ch_ref, st_ref, w_ref, y_ref, nst_ref):
    C = cb_ref.shape[1]
    x = cc_ref[...] * ch_ref[...]
    s0 = st_ref[:, :C]
    s1 = st_ref[:, C:]
    zc = s0 * w_ref[0:1, :] + s1 * w_ref[1:2, :] + x * w_ref[2:3, :]
    y_ref[...] = (cb_ref[...] * zc).astype(y_ref.dtype)
    nst_ref[:, :C] = s1
    nst_ref[:, C:] = x


def _conv_decode(z, state2d, w_conv, layer):
    Bd = z.shape[0]
    C = CONV_DIM
    blocks = 3 * _nbytes((Bd, C), F32) + 4 * _nbytes((Bd, C), F32) + _nbytes((Bd, C), BF16)
    return pl.pallas_call(
        _conv_decode_kernel,
        grid=(1,),
        in_specs=[pl.BlockSpec((Bd, C), lambda i: (0, Z_CB // C)),
                  pl.BlockSpec((Bd, C), lambda i: (0, Z_CC // C)),
                  pl.BlockSpec((Bd, C), lambda i: (0, Z_CH // C)),
                  pl.BlockSpec((None, Bd, 2 * C), lambda i: (layer, 0, 0)),
                  pl.BlockSpec((None, 3, C), lambda i: (layer, 0, 0))],
        out_specs=[pl.BlockSpec((Bd, C), lambda i: (0, 0)),
                   pl.BlockSpec((Bd, 2 * C), lambda i: (0, 0))],
        out_shape=[jax.ShapeDtypeStruct((Bd, C), BF16),
                   jax.ShapeDtypeStruct((Bd, 2 * C), F32)],
        compiler_params=_params(blocks),
        name="conv_decode",
    )(z, z, z, state2d, w_conv)


def _mla_prep_kernel(qa_ref, kvg_ref, gqa_ref, gkv_ref, wq_ref, wk_ref, wv_ref,
                     cq_ref, sq_ref, ck_ref, sk_ref, *out_refs, with_kv):
    q_ref, rows_ref = out_refs[0], out_refs[1]
    HP = MLA_HEADS * HEAD_PAD
    cq = _rmsn(qa_ref[...], gqa_ref[...]).astype(BF16)
    qq = _dot(cq, wq_ref[...])
    c_q, s_q = cq_ref[...], sq_ref[...]
    for h in range(MLA_HEADS):
        lo = h * HEAD_PAD
        q_ref[:, lo:lo + HEAD_PAD] = (qq[:, lo:lo + HEAD_PAD] * c_q
                                      + qq[:, HP + lo:HP + lo + HEAD_PAD] * s_q).astype(q_ref.dtype)
    ckv = _rmsn(kvg_ref[:, :MLA_KV_RANK], gkv_ref[...])
    kr = kvg_ref[:, MLA_KV_RANK:MLA_KV_RANK + LANES]
    lane = lax.broadcasted_iota(jnp.int32, kr.shape, 1)
    half = MLA_ROPE // 2
    partner = jnp.where(lane < half, pltpu.roll(kr, LANES - half, axis=1), pltpu.roll(kr, half, axis=1))
    kr = kr * ck_ref[...] + partner * sk_ref[...]
    rows_ref[:, :MLA_KV_RANK] = ckv
    rows_ref[:, MLA_KV_RANK:] = kr[:, :MLA_ROPE]
    if with_kv:
        k_ref, v_ref = out_refs[2], out_refs[3]
        ckv_b = ckv.astype(BF16)
        kin = jnp.concatenate([ckv_b, kr.astype(BF16)], axis=1)
        k_ref[...] = _dot(kin, wk_ref[...]).astype(k_ref.dtype)
        v_ref[...] = _dot(ckv_b, wv_ref[...]).astype(v_ref.dtype)


def _mla_prep(z, W, layer, tabs, T, *, with_kv):
    M = z.shape[0]
    tm = _pick(min(M, T), 512, 16)
    nt = max(T // tm, 1)
    HP = MLA_HEADS * HEAD_PAD
    HV = MLA_HEADS * MLA_V
    KIN = MLA_KV_RANK + LANES
    tab_spec = pl.BlockSpec((tm, LANES), lambda i: (i % nt, 0))
    out_specs = [pl.BlockSpec((tm, HP), lambda i: (i, 0)),
                 pl.BlockSpec((tm, MLA_CACHE_DIM), lambda i: (i, 0))]
    out_shape = [jax.ShapeDtypeStruct((M, HP), BF16),
                 jax.ShapeDtypeStruct((M, MLA_CACHE_DIM), F32)]
    if with_kv:
        out_specs += [pl.BlockSpec((tm, HP), lambda i: (i, 0)),
                      pl.BlockSpec((tm, HV), lambda i: (i, 0))]
        out_shape += [jax.ShapeDtypeStruct((M, HP), BF16),
                      jax.ShapeDtypeStruct((M, HV), BF16)]
    blocks = (_nbytes((tm, MLA_Q_RANK + KVG_W), F32) + _nbytes((MLA_Q_RANK, 2 * HP), BF16)
              + _nbytes((KIN, HP), BF16) + _nbytes((MLA_KV_RANK, HV), BF16)
              + 4 * _nbytes((tm, LANES), F32) + _nbytes((tm, 2 * HP + HV), BF16)
              + _nbytes((tm, 3 * LANES), F32) + _nbytes((tm, 2 * HP), F32))
    return pl.pallas_call(
        functools.partial(_mla_prep_kernel, with_kv=with_kv),
        grid=(M // tm,),
        in_specs=[pl.BlockSpec((tm, MLA_Q_RANK), lambda i: (i, Z_QA // MLA_Q_RANK)),
                  pl.BlockSpec((tm, KVG_W), lambda i: (i, Z_KVG // KVG_W)),
                  pl.BlockSpec((None, 1, MLA_Q_RANK), lambda i: (layer, 0, 0)),
                  pl.BlockSpec((None, 1, MLA_KV_RANK), lambda i: (layer, 0, 0)),
                  pl.BlockSpec((None, MLA_Q_RANK, 2 * HP), lambda i: (layer, 0, 0)),
                  pl.BlockSpec((None, KIN, HP), lambda i: (layer, 0, 0)),
                  pl.BlockSpec((None, MLA_KV_RANK, HV), lambda i: (layer, 0, 0)),
                  tab_spec, tab_spec, tab_spec, tab_spec],
        out_specs=out_specs,
        out_shape=out_shape,
        compiler_params=_params(blocks, 0, ("parallel",)),
        name="mla_prep" if with_kv else "mla_prep_decode",
    )(z, z, W["g_qa"], W["g_kva"], W["wq2"], W["wk"], W["wv"], *tabs)


def _mla_attn_kernel(q_ref, k_ref, v_ref, o_ref, *, tq, tk):
    qi = pl.program_id(2)
    nr = tq // tk
    c = (MLA_QK ** -0.5) * 1.4426950408889634
    vlane = lax.broadcasted_iota(jnp.int32, (tk, LANES), 1)
    olane = lax.broadcasted_iota(jnp.int32, (tk, LANES), 1)
    tri = lax.broadcasted_iota(jnp.int32, (tk, tk), 1) <= lax.broadcasted_iota(jnp.int32, (tk, tk), 0)

    def scores(kb, first_row):
        ks = pl.multiple_of(kb * tk, tk)
        rows = slice(first_row * tk, tq)
        return (_dot_nt(q_ref[rows, :HEAD_PAD], k_ref[pl.ds(ks, tk), :HEAD_PAD]),
                _dot_nt(q_ref[rows, HEAD_PAD:], k_ref[pl.ds(ks, tk), HEAD_PAD:]))

    def head(s_all, vt, ms, accs, first_row, masked_row):
        new_ms, alphas, ps = [], [], []
        for i, r in enumerate(range(first_row, nr)):
            s = s_all[i * tk:(i + 1) * tk]
            if r == masked_row:
                s = jnp.where(tri, s, -jnp.inf)
            m_new = jnp.maximum(ms[i], jnp.max(s, axis=-1, keepdims=True))
            alphas.append(jnp.exp2((ms[i] - m_new) * c))
            ps.append(jnp.exp2((s - m_new) * c).astype(BF16))
            new_ms.append(m_new)
        pv = _dot(ps[0] if len(ps) == 1 else jnp.concatenate(ps, axis=0), vt)
        new_accs = [alphas[i] * accs[i] + pv[i * tk:(i + 1) * tk] for i in range(len(ps))]
        return new_ms, new_accs

    def block(kb, state, s_both, first_row, masked_row):
        ks = pl.multiple_of(kb * tk, tk)
        v = v_ref[pl.ds(ks, tk), :].astype(F32)
        v0 = jnp.where(vlane < MLA_V, v, jnp.where(vlane == MLA_V, 1.0, 0.0)).astype(BF16)
        v1 = jnp.where(vlane >= MLA_V, v, jnp.where(vlane == 0, 1.0, 0.0)).astype(BF16)
        live = state[first_row:]
        m0, a0 = head(s_both[0], v0, [t[0] for t in live], [t[1] for t in live], first_row, masked_row)
        m1, a1 = head(s_both[1], v1, [t[2] for t in live], [t[3] for t in live], first_row, masked_row)
        return state[:first_row] + tuple(zip(m0, a0, m1, a1))

    neg = jnp.full((tk, 1), -jnp.inf, F32)
    nil = jnp.zeros((tk, LANES), F32)
    state = tuple((neg, nil, neg, nil) for _ in range(nr))
    base = qi * nr

    def full_blocks(i, st):
        for u in range(nr):
            st = block(i * nr + u, st, scores(i * nr + u, 0), 0, -1)
        return st

    state = lax.fori_loop(0, qi, full_blocks, state)
    for d in range(nr):
        state = block(base + d, state, scores(base + d, d), d, d)
    for r in range(nr):
        m0, a0, m1, a1 = state[r]
        out = jnp.where(olane < MLA_V, a0 / a0[:, MLA_V:MLA_V + 1], a1 / a1[:, 0:1])
        o_ref[r * tk:(r + 1) * tk, :] = out.astype(o_ref.dtype)


def _mla_attn(q, k, v, B, T):
    tq = _pick(T, 1024, 16)
    tk = _pick(T, 256, 16)
    assert tq % tk == 0
    nq = T // tq
    blocks = (_nbytes((tq, 2 * HEAD_PAD), BF16) + _nbytes((T, 2 * HEAD_PAD), BF16)
              + _nbytes((T, LANES), BF16) + _nbytes((tq, LANES), BF16))
    return pl.pallas_call(
        functools.partial(_mla_attn_kernel, tq=tq, tk=tk),
        grid=(B, MLA_HEADS // 2, nq),
        in_specs=[pl.BlockSpec((tq, 2 * HEAD_PAD), lambda b, hp, i: (b * nq + i, hp)),
                  pl.BlockSpec((T, 2 * HEAD_PAD), lambda b, hp, i: (b, hp)),
                  pl.BlockSpec((T, LANES), lambda b, hp, i: (b, hp))],
        out_specs=pl.BlockSpec((tq, LANES), lambda b, hp, i: (b * nq + i, hp)),
        out_shape=jax.ShapeDtypeStruct((B * T, MLA_HEADS * MLA_V), BF16),
        compiler_params=_params(blocks, 8 << 20, ("parallel", "parallel", "arbitrary")),
        name="mla_attn",
    )(q, k, v)


def _absorb_kernel(q_ref, w_ref, o_ref):
    o_ref[...] = _dot(q_ref[...], w_ref[...]).astype(o_ref.dtype)


def _absorb(q, wabs, layer):
    Bd = q.shape[0]
    QC = wabs.shape[-1]
    return pl.pallas_call(
        _absorb_kernel,
        grid=(MLA_HEADS,),
        in_specs=[pl.BlockSpec((Bd, HEAD_PAD), lambda h: (0, h)),
                  pl.BlockSpec((None, None, HEAD_PAD, QC), lambda h: (layer, h, 0, 0))],
        out_specs=pl.BlockSpec((Bd, QC), lambda h: (0, h)),
        out_shape=jax.ShapeDtypeStruct((Bd, MLA_HEADS * QC), BF16),
        name="mla_absorb",
    )(q, wabs)


def _page_copy(cache_ref, buf_ref, sem_ref, layer, page, slot, j, page_size):
    return pltpu.make_async_copy(cache_ref.at[layer, page],
                                 buf_ref.at[slot, :, pl.ds(j * page_size, page_size)],
                                 sem_ref.at[slot])


def _dec_attn_kernel(pt_ref, q_ref, new_ref, cache_ref, o_ref, buf_ref, kb_ref, sem_ref,
                     *, layer, n_pages, page_size, chunk):
    b = pl.program_id(0)
    nb = pl.num_programs(0)
    scale = MLA_QK ** -0.5
    R = MLA_KV_RANK

    def fetch(bb, slot):
        for j in range(n_pages):
            _page_copy(cache_ref, buf_ref, sem_ref, layer, pt_ref[bb * n_pages + j], slot, j,
                       page_size).start()

    @pl.when(b == 0)
    def _():
        fetch(0, 0)

    @pl.when(b + 1 < nb)
    def _():
        fetch(b + 1, (b + 1) % 2)

    slot = b % 2
    for j in range(n_pages):
        _page_copy(cache_ref, buf_ref, sem_ref, layer, 0, slot, j, page_size).wait()

    q = q_ref[0]
    q_lat = q[:, :R]
    q_rope = q[:, R:R + MLA_ROPE]
    n_chunks = (n_pages * page_size) // chunk
    parts = []
    for c in range(n_chunks):
        cols = slice(c * chunk, (c + 1) * chunk)
        rows = buf_ref[slot, :, cols].astype(BF16)
        kb_ref[:, cols] = rows[:R]
        parts.append(_dot(q_lat, rows[:R]) + _dot(q_rope, rows[R:]))
    s = jnp.concatenate(parts, axis=1)
    new = new_ref[0].astype(BF16).astype(F32)
    s_new = jnp.sum(q.astype(F32)[:, :MLA_CACHE_DIM] * new, axis=-1, keepdims=True)
    m = jnp.maximum(jnp.max(s, axis=-1, keepdims=True), s_new)
    p = jnp.exp((s - m) * scale)
    p_new = jnp.exp((s_new - m) * scale)
    l = jnp.sum(p, axis=-1, keepdims=True) + p_new
    acc = p_new * new[:, :R]
    pb = p.astype(BF16)
    for c in range(n_chunks):
        cols = slice(c * chunk, (c + 1) * chunk)
        acc = acc + _dot_nt(pb[:, cols], kb_ref[:, cols])
    o_ref[0] = acc / l


def _dec_attn(pt_flat, qcat, rows_new, cache_t, layer, n_pages):
    Bd = rows_new.shape[0]
    page_size = cache_t.shape[3]
    L = n_pages * page_size
    QC = qcat.shape[-1] // MLA_HEADS
    chunk = _pick(L, 1024, page_size)
    q3 = qcat.reshape(Bd, MLA_HEADS, QC)
    new3 = rows_new.reshape(Bd, 1, MLA_CACHE_DIM)
    scratch = 2 * _nbytes((MLA_CACHE_DIM, L), F32) + _nbytes((MLA_KV_RANK, L), BF16)
    grid_spec = pltpu.PrefetchScalarGridSpec(
        num_scalar_prefetch=1,
        grid=(Bd,),
        in_specs=[pl.BlockSpec((1, MLA_HEADS, QC), lambda b, pt: (b, 0, 0)),
                  pl.BlockSpec((1, 1, MLA_CACHE_DIM), lambda b, pt: (b, 0, 0)),
                  pl.BlockSpec(memory_space=pl.ANY)],
        out_specs=pl.BlockSpec((1, MLA_HEADS, MLA_KV_RANK), lambda b, pt: (b, 0, 0)),
        scratch_shapes=[pltpu.VMEM((2, MLA_CACHE_DIM, L), F32),
                        pltpu.VMEM((MLA_KV_RANK, L), BF16),
                        pltpu.SemaphoreType.DMA((2,))],
    )
    return pl.pallas_call(
        functools.partial(_dec_attn_kernel, layer=layer, n_pages=n_pages, page_size=page_size,
                          chunk=chunk),
        grid_spec=grid_spec,
        out_shape=jax.ShapeDtypeStruct((Bd, MLA_HEADS, MLA_KV_RANK), F32),
        compiler_params=_params(1 << 20, scratch, ("arbitrary",)),
        name="mla_decode_attn",
    )(pt_flat, q3, new3, cache_t)


def _dec_out_kernel(o_ref, w_ref, y_ref):
    y_ref[...] = _dot(o_ref[...].astype(BF16), w_ref[...]).astype(y_ref.dtype)


def _dec_out(o_lat2d, wuv_bd, layer):
    Bd = o_lat2d.shape[0]
    return pl.pallas_call(
        _dec_out_kernel,
        grid=(MLA_HEADS // 2,),
        in_specs=[pl.BlockSpec((Bd, 2 * MLA_KV_RANK), lambda hp: (0, hp)),
                  pl.BlockSpec((None, None, 2 * MLA_KV_RANK, 2 * MLA_V), lambda hp: (layer, hp, 0, 0))],
        out_specs=pl.BlockSpec((Bd, 2 * MLA_V), lambda hp: (0, hp)),
        out_shape=jax.ShapeDtypeStruct((Bd, MLA_HEADS * MLA_V), BF16),
        name="mla_decode_out",
    )(o_lat2d, wuv_bd)


def _log_decay(gg, wg, bg):
    g_pre = _dot(gg.astype(BF16), wg) + bg
    return (jnp.minimum(g_pre, 0.0) - jnp.log1p(jnp.exp(-jnp.abs(g_pre)))) / GLA_TAU


def _head_out(o, g_o, gr):
    return _rmsn(o, g_o) * (gr * jax.nn.sigmoid(gr))


def _gla_masks():
    C = GLA_CHUNK
    row = lax.broadcasted_iota(jnp.int32, (C, C), 0)
    col = lax.broadcasted_iota(jnp.int32, (C, C), 1)
    level = jnp.zeros((C, C), jnp.int32)
    for n, size in enumerate(GLA_LEVELS):
        sh = size.bit_length() - 1
        rb, cb = row >> sh, col >> sh
        level = jnp.where(rb - cb == 1, jnp.where((rb & 1) == 1, n + 1, level), level)
    rel = col - ((row >> 3) << 3)
    return level, rel, row >= col


def _gla_scores(q, k, b, level, rel, causal):
    C, DK = q.shape
    a = jnp.zeros((C, C), F32)
    for n, size in enumerate(GLA_LEVELS):
        ref = jnp.concatenate(
            [jnp.broadcast_to(b[p + size - 1:p + size, :], (2 * size, DK))
             for p in range(0, C, 2 * size)], axis=0)
        e = jnp.exp(-jnp.abs(b - ref))
        a = jnp.where(level == n + 1, _dot_nt((q * e).astype(BF16), (k * e).astype(BF16)), a)
    G = C // GLA_DIAG
    k3 = k.reshape(G, GLA_DIAG, DK)
    b3 = b.reshape(G, GLA_DIAG, DK)
    for j in range(GLA_DIAG):
        kj = jnp.broadcast_to(k3[:, j:j + 1, :], (G, GLA_DIAG, DK)).reshape(C, DK)
        bj = jnp.broadcast_to(b3[:, j:j + 1, :], (G, GLA_DIAG, DK)).reshape(C, DK)
        w = q * kj * jnp.exp(b - bj)
        a = jnp.where(rel == j, jnp.sum(w, axis=-1, keepdims=True), a)
    return jnp.where(causal, a, 0.0)


def _gla_prompt_kernel(gq_ref, gk_ref, gv_ref, gr_ref, gg_ref, wg_ref, bg_ref, go_ref,
                       y_ref, s_ref, la_ref):
    T = gq_ref.shape[0]
    C = GLA_CHUNK
    la_ref[...] = _log_decay(gg_ref[...], wg_ref[...], bg_ref[...])
    level, rel, causal = _gla_masks()
    tri = causal.astype(F32)
    g_o = go_ref[...]

    def chunk(c, st):
        r0 = pl.multiple_of(c * C, C)
        q = gq_ref[pl.ds(r0, C), :] * (GLA_DK ** -0.5)
        k = gk_ref[pl.ds(r0, C), :]
        v = gv_ref[pl.ds(r0, C), :].astype(BF16)
        b = jnp.dot(tri, la_ref[pl.ds(r0, C), :], precision=lax.Precision.HIGHEST,
                    preferred_element_type=F32)
        b_last = b[C - 1:C, :]
        a = _gla_scores(q, k, b, level, rel, causal)
        o = _dot_nt((q * jnp.exp(b)).astype(BF16), st.astype(BF16)) + _dot(a.astype(BF16), v)
        y_ref[pl.ds(r0, C), :] = _head_out(o, g_o, gr_ref[pl.ds(r0, C), :]).astype(y_ref.dtype)
        kd = (k * jnp.exp(b_last - b)).astype(BF16)
        return st * jnp.exp(b_last) + _dot_tn(v, kd)

    st = lax.fori_loop(0, T // C, chunk, jnp.zeros((GLA_DV, GLA_DK), F32), unroll=8)
    s_ref[...] = st.T


def _gla_prompt(z, W, layer, B, T):
    DK, DV, H = GLA_DK, GLA_DV, GLA_HEADS
    blocks = (3 * _nbytes((T, DK), F32) + 2 * _nbytes((T, DV), F32) + _nbytes((T, DV), BF16)
              + _nbytes((DK, DV), F32))
    scratch = _nbytes((T, DK), F32)
    return pl.pallas_call(
        _gla_prompt_kernel,
        grid=(B, H),
        in_specs=[pl.BlockSpec((T, DK), lambda b, h: (b, Z_GQ // DK + h)),
                  pl.BlockSpec((T, DK), lambda b, h: (b, Z_GK // DK + h)),
                  pl.BlockSpec((T, DV), lambda b, h: (b, Z_GV // DV + h)),
                  pl.BlockSpec((T, DV), lambda b, h: (b, Z_GR // DV + h)),
                  pl.BlockSpec((T, LANES), lambda b, h: (b, Z_GG // LANES)),
                  pl.BlockSpec((None, LANES, DK), lambda b, h: (layer, 0, h)),
                  pl.BlockSpec((None, 1, DK), lambda b, h: (layer, 0, h)),
                  pl.BlockSpec((None, 1, DV), lambda b, h: (layer, 0, 0))],
        out_specs=[pl.BlockSpec((T, DV), lambda b, h: (b, h)),
                   pl.BlockSpec((None, None, DK, DV), lambda b, h: (b, h, 0, 0))],
        out_shape=[jax.ShapeDtypeStruct((B * T, H * DV), BF16),
                   jax.ShapeDtypeStruct((B, H, DK, DV), F32)],
        scratch_shapes=[pltpu.VMEM((T, DK), F32)],
        compiler_params=_params(blocks, scratch, ("parallel", "parallel")),
        name="gla_prompt",
    )(z, z, z, z, z, W["wg2"], W["b_gla_g"], W["g_gla_o"])


def _gla_decode_kernel(gq_ref, gk_ref, gv_ref, gr_ref, gg_ref, s_ref, wg_ref, bg_ref, go_ref,
                       y_ref, ns_ref, o_scr):
    nb = gq_ref.shape[0]
    DK, DV, H = GLA_DK, GLA_DV, GLA_HEADS
    la = _log_decay(gg_ref[...], wg_ref[...], bg_ref[...])
    a = jnp.exp(la)
    q = gq_ref[...] * (GLA_DK ** -0.5)
    k = gk_ref[...]
    v = gv_ref[...]
    qa = q * a
    pieces = [x[:, h * DK:(h + 1) * DK] for x in (a, k, qa) for h in range(H)]
    pad = LANES - 3 * H * nb
    stack = jnp.concatenate(pieces + [jnp.zeros((pad, DK), F32)], axis=0)
    cols = stack.T
    qk = q * k
    for h in range(H):
        att = jnp.sum(qk[:, h * DK:(h + 1) * DK], axis=-1, keepdims=True)
        o_scr[:, h * DV:(h + 1) * DV] = att * v[:, h * DV:(h + 1) * DV]
    for i in range(nb):
        for h in range(H):
            c = h * nb + i
            a_col = cols[:, c:c + 1]
            k_col = cols[:, H * nb + c:H * nb + c + 1]
            qa_col = cols[:, 2 * H * nb + c:2 * H * nb + c + 1]
            s_old = s_ref[i, h]
            v_row = v[i:i + 1, h * DV:(h + 1) * DV]
            ns_ref[i, h] = a_col * s_old + k_col * v_row
            o_scr[i:i + 1, h * DV:(h + 1) * DV] += jnp.sum(qa_col * s_old, axis=0, keepdims=True)
    g_o = go_ref[...]
    gr = gr_ref[...]
    for h in range(H):
        sl = slice(h * DV, (h + 1) * DV)
        y_ref[:, sl] = _head_out(o_scr[:, sl], g_o, gr[:, sl]).astype(y_ref.dtype)


def _gla_decode(z, state, W, layer):
    Bd = z.shape[0]
    DK, DV, H = GLA_DK, GLA_DV, GLA_HEADS
    nb = 8
    HK, HV = H * DK, H * DV
    blocks = (2 * _nbytes((nb, HK), F32) + 3 * _nbytes((nb, HV), F32) + _nbytes((nb, LANES), F32)
              + 2 * _nbytes((nb, H, DK, DV), F32) + _nbytes((LANES, HK), BF16))
    return pl.pallas_call(
        _gla_decode_kernel,
        grid=(Bd // nb,),
        in_specs=[pl.BlockSpec((nb, HK), lambda i: (i, Z_GQ // HK)),
                  pl.BlockSpec((nb, HK), lambda i: (i, Z_GK // HK)),
                  pl.BlockSpec((nb, HV), lambda i: (i, Z_GV // HV)),
                  pl.BlockSpec((nb, HV), lambda i: (i, Z_GR // HV)),
                  pl.BlockSpec((nb, LANES), lambda i: (i, Z_GG // LANES)),
                  pl.BlockSpec((None, nb, H, DK, DV), lambda i: (layer, i, 0, 0, 0)),
                  pl.BlockSpec((None, LANES, HK), lambda i: (layer, 0, 0)),
                  pl.BlockSpec((None, 1, HK), lambda i: (layer, 0, 0)),
                  pl.BlockSpec((None, 1, DV), lambda i: (layer, 0, 0))],
        out_specs=[pl.BlockSpec((nb, HV), lambda i: (i, 0)),
                   pl.BlockSpec((nb, H, DK, DV), lambda i: (i, 0, 0, 0))],
        out_shape=[jax.ShapeDtypeStruct((Bd, HV), F32),
                   jax.ShapeDtypeStruct((Bd, H, DK, DV), F32)],
        scratch_shapes=[pltpu.VMEM((nb, HV), F32)],
        compiler_params=_params(blocks, 0, ("parallel",)),
        name="gla_decode",
    )(z, z, z, z, z, state, W["wg2"], W["b_gla_g"], W["g_gla_o"])


def _merge_kernel(yc_ref, ym_ref, yg_ref, gc_ref, gm_ref, gg_ref, wc_ref, wm_ref, wg_ref, o_ref):
    tm = o_ref.shape[0]
    rs = min(ROW_SUB, tm)
    for r in range(0, tm, rs):
        rows = slice(r, r + rs)
        acc = jax.nn.sigmoid(gc_ref[rows, :]) * _dot(yc_ref[rows, :].astype(BF16), wc_ref[...])
        acc = acc + jax.nn.sigmoid(gm_ref[rows, :]) * _dot(ym_ref[rows, :].astype(BF16), wm_ref[...])
        acc = acc + jax.nn.sigmoid(gg_ref[rows, :]) * _dot(yg_ref[rows, :].astype(BF16), wg_ref[...])
        o_ref[rows, :] = acc.astype(o_ref.dtype)


def _merge(y_conv, y_mla, y_gla, z, W, layer):
    M = z.shape[0]
    tm = _pick(M, 1024, 16)
    tn = 512
    C = y_conv.shape[1]
    ga = Z_GA // tn
    gstep = D_MODEL // tn
    blocks = (3 * _nbytes((tm, C), F32) + 3 * _nbytes((tm, tn), F32) + 3 * _nbytes((C, tn), BF16)
              + _nbytes((tm, tn), BF16))
    y_spec = pl.BlockSpec((tm, C), lambda i, j: (i, 0))
    w_spec = pl.BlockSpec((None, C, tn), lambda i, j: (layer, 0, j))
    return pl.pallas_call(
        _merge_kernel,
        grid=(M // tm, D_MODEL // tn),
        in_specs=[y_spec, y_spec, y_spec,
                  pl.BlockSpec((tm, tn), lambda i, j: (i, ga + j)),
                  pl.BlockSpec((tm, tn), lambda i, j: (i, ga + gstep + j)),
                  pl.BlockSpec((tm, tn), lambda i, j: (i, ga + 2 * gstep + j)),
                  w_spec, w_spec, w_spec],
        out_specs=pl.BlockSpec((tm, tn), lambda i, j: (i, j)),
        out_shape=jax.ShapeDtypeStruct((M, D_MODEL), BF16),
        compiler_params=_params(blocks, 0, ("parallel", "arbitrary")),
        name="merge",
    )(y_conv, y_mla, y_gla, z, z, z, W["w_br_conv"], W["w_br_mla"], W["w_br_gla"])


def _ffn_up_kernel(h_ref, halo_ref, g_ref, wg_ref, wv_ref, cg_ref, cv_ref,
                   act_ref, sg_ref, sv_ref, xn_ref, *, tiles_per_seq):
    tm = h_ref.shape[0]
    i = pl.program_id(0)

    @pl.when(pl.program_id(1) == 0)
    def _():
        g = g_ref[...]
        halo = _rmsn(halo_ref[...], g)
        halo = jnp.where(i % tiles_per_seq == 0, 0.0, halo)
        xn_ref[:HALO, :] = halo.astype(BF16)
        xn_ref[HALO:, :] = _rmsn(h_ref[...], g).astype(BF16)

    xn = xn_ref[...]

    def conv(u, c_ref, sl):
        out = pltpu.roll(u, 2, axis=0) * c_ref[0:1, sl]
        out = out + pltpu.roll(u, 1, axis=0) * c_ref[1:2, sl]
        out = out + u * c_ref[2:3, sl]
        return out[HALO:]

    for c in range(0, act_ref.shape[1], COL_SUB):
        sl = slice(c, c + COL_SUB)
        ug = _dot(xn, wg_ref[:, sl])
        uv = _dot(xn, wv_ref[:, sl])
        gate = conv(ug, cg_ref, sl)
        val = conv(uv, cv_ref, sl)
        act_ref[:, sl] = (gate * jax.nn.sigmoid(gate) * val).astype(act_ref.dtype)
        sg_ref[:, sl] = ug[HALO + tm - 2:, :]
        sv_ref[:, sl] = uv[HALO + tm - 2:, :]


def _ffn_up_prompt(h, W, layer, B, T):
    M, K = h.shape
    tm = _pick(T, 1024, HALO)
    tn = _pick(D_FF, 512, LANES)
    nt = T // tm
    nj = D_FF // tn
    blocks = (_nbytes((tm + HALO, K), F32) + 2 * _nbytes((K, tn), BF16) + _nbytes((tm, tn), BF16)
              + 4 * _nbytes((tm + HALO, tn), F32))
    scratch = _nbytes((tm + HALO, K), BF16)
    hb = tm // HALO
    act, sg, sv = pl.pallas_call(
        functools.partial(_ffn_up_kernel, tiles_per_seq=nt),
        grid=(M // tm, nj),
        in_specs=[pl.BlockSpec((tm, K), lambda i, j: (i, 0)),
                  pl.BlockSpec((HALO, K), lambda i, j: (jnp.maximum(i * hb - 1, 0), 0)),
                  pl.BlockSpec((None, 1, K), lambda i, j: (layer, 0, 0)),
                  pl.BlockSpec((None, K, tn), lambda i, j: (layer, 0, j)),
                  pl.BlockSpec((None, K, tn), lambda i, j: (layer, 0, nj + j)),
                  pl.BlockSpec((None, 3, tn), lambda i, j: (layer, 0, j)),
                  pl.BlockSpec((None, 3, tn), lambda i, j: (layer, 0, nj + j))],
        out_specs=[pl.BlockSpec((tm, tn), lambda i, j: (i, j)),
                   pl.BlockSpec((None, 2, tn), lambda i, j: (i, 0, j)),
                   pl.BlockSpec((None, 2, tn), lambda i, j: (i, 0, j))],
        out_shape=[jax.ShapeDtypeStruct((M, D_FF), BF16),
                   jax.ShapeDtypeStruct((M // tm, 2, D_FF), F32),
                   jax.ShapeDtypeStruct((M // tm, 2, D_FF), F32)],
        scratch_shapes=[pltpu.VMEM((tm + HALO, K), BF16)],
        compiler_params=_params(blocks, scratch, ("parallel", "arbitrary")),
        name="ffn_up",
    )(h, h, W["g_ffn"], W["w_up"], W["w_up"], W["w_ffn_conv"], W["w_ffn_conv"])
    return act, sg[nt - 1::nt], sv[nt - 1::nt]


def _ffn_conv_decode_kernel(ug_ref, uv_ref, s0g_ref, s0v_ref, s1g_ref, s1v_ref, cg_ref, cv_ref,
                            act_ref):
    gate = s0g_ref[...] * cg_ref[0:1, :] + s1g_ref[...] * cg_ref[1:2, :] + ug_ref[...] * cg_ref[2:3, :]
    val = s0v_ref[...] * cv_ref[0:1, :] + s1v_ref[...] * cv_ref[1:2, :] + uv_ref[...] * cv_ref[2:3, :]
    act_ref[...] = (gate * jax.nn.sigmoid(gate) * val).astype(act_ref.dtype)


def _ffn_conv_decode(u, state2d, w_ffn_conv, layer):
    Bd = u.shape[0]
    tn = _pick(D_FF, 1024, LANES)
    nj = D_FF // tn
    blocks = 6 * _nbytes((Bd, tn), F32) + _nbytes((Bd, tn), BF16)
    u_spec = lambda off: pl.BlockSpec((Bd, tn), lambda j: (0, off + j))
    s_spec = lambda off: pl.BlockSpec((None, Bd, tn), lambda j: (layer, 0, off + j))
    c_spec = lambda off: pl.BlockSpec((None, 3, tn), lambda j: (layer, 0, off + j))
    return pl.pallas_call(
        _ffn_conv_decode_kernel,
        grid=(nj,),
        in_specs=[u_spec(0), u_spec(nj), s_spec(0), s_spec(nj), s_spec(2 * nj), s_spec(3 * nj),
                  c_spec(0), c_spec(nj)],
        out_specs=pl.BlockSpec((Bd, tn), lambda j: (0, j)),
        out_shape=jax.ShapeDtypeStruct((Bd, D_FF), BF16),
        compiler_params=_params(blocks, 0, ("parallel",)),
        name="ffn_conv_decode",
    )(u, u, state2d, state2d, state2d, state2d, w_ffn_conv, w_ffn_conv)


def _ple_kernel(h_ref, g_ref, wg_ref, p_ref, wp_ref, o_ref, xn_ref):
    tn = o_ref.shape[1]
    j = pl.program_id(1)

    @pl.when(j == 0)
    def _():
        xn_ref[...] = _rmsn(h_ref[...], g_ref[...]).astype(BF16)

    tm = o_ref.shape[0]
    rs = min(ROW_SUB, tm)
    for r in range(0, tm, rs):
        rows = slice(r, r + rs)
        gate = jax.nn.sigmoid(_dot(xn_ref[rows, :], wg_ref[...]))
        emb = _dot(p_ref[rows, :].astype(BF16), wp_ref[...])
        res = h_ref[rows, pl.ds(pl.multiple_of(j * tn, tn), tn)]
        o_ref[rows, :] = res + gate * emb


def _ple(h, p, W, layer):
    M, K = h.shape
    tm = _pick(M, 1024, 16)
    tn = 512
    blocks = (_nbytes((tm, K), F32) + _nbytes((K, tn), BF16) + _nbytes((tm, PLE_DIM), F32)
              + _nbytes((PLE_DIM, tn), BF16) + _nbytes((tm, tn), F32))
    return pl.pallas_call(
        _ple_kernel,
        grid=(M // tm, K // tn),
        in_specs=[pl.BlockSpec((tm, K), lambda i, j: (i, 0)),
                  pl.BlockSpec((None, 1, K), lambda i, j: (layer, 0, 0)),
                  pl.BlockSpec((None, K, tn), lambda i, j: (layer, 0, j)),
                  pl.BlockSpec((None, tm, PLE_DIM), lambda i, j: (layer, i, 0)),
                  pl.BlockSpec((None, PLE_DIM, tn), lambda i, j: (layer, 0, j))],
        out_specs=pl.BlockSpec((tm, tn), lambda i, j: (i, j)),
        out_shape=jax.ShapeDtypeStruct((M, K), F32),
        scratch_shapes=[pltpu.VMEM((tm, K), BF16)],
        compiler_params=_params(blocks, _nbytes((tm, K), BF16), ("parallel", "arbitrary")),
        name="ple",
    )(h, W["g_ple"], W["w_ple_gate"], p, W["w_ple"])


def _final_norm_kernel(h_ref, g_ref, o_ref):
    o_ref[...] = _rmsn(h_ref[...], g_ref[...])


def _final_norm(h, g):
    M, K = h.shape
    tm = _pick(M, 512)
    return pl.pallas_call(
        _final_norm_kernel,
        grid=(M // tm,),
        in_specs=[pl.BlockSpec((tm, K), lambda i: (i, 0)),
                  pl.BlockSpec((1, K), lambda i: (0, 0))],
        out_specs=pl.BlockSpec((tm, K), lambda i: (i, 0)),
        out_shape=jax.ShapeDtypeStruct((M, K), F32),
        compiler_params=_params(2 * _nbytes((tm, K), F32), 0, ("parallel",)),
        name="final_norm",
    )(h, g)


def _w_in_segments():
    offs = [0]
    for s in IN_SPLITS:
        offs.append(offs[-1] + s)
    dst = {0: Z_CB, 1: Z_CC, 2: Z_CH, 10: Z_GA, 11: Z_GA + D_MODEL, 12: Z_GA + 2 * D_MODEL,
           7: Z_GV, 9: Z_GR, 3: Z_QA, 5: Z_GQ, 6: Z_GK, 4: Z_KVG, 8: Z_GG}
    return [(offs[n], IN_SPLITS[n], d) for n, d in dst.items()]


def _w_in_relayout_kernel(src_ref, valid_ref, w_ref, o_ref):
    t = w_ref[0].T
    col = lax.broadcasted_iota(jnp.int32, t.shape, 1)
    o_ref[...] = jnp.where(col < valid_ref[pl.program_id(1)], t, 0.0).astype(o_ref.dtype)


def _w_in_relayout(w_in):
    depth, K, N = w_in.shape
    w_t = jnp.swapaxes(w_in, 1, 2)
    src = [0] * (Z_DIM // LANES)
    valid = [0] * (Z_DIM // LANES)
    for s, width, d in _w_in_segments():
        assert s % 8 == 0 and d % LANES == 0
        for o in range(0, width, LANES):
            src[(d + o) // LANES] = (s + o) // 8
            valid[(d + o) // LANES] = min(LANES, width - o)
    assert max(src) * 8 + LANES <= N
    blocks = _nbytes((LANES, K), F32) + _nbytes((K, LANES), BF16) + _nbytes((K, LANES), F32)
    grid_spec = pltpu.PrefetchScalarGridSpec(
        num_scalar_prefetch=2,
        grid=(depth, Z_DIM // LANES),
        in_specs=[pl.BlockSpec((pl.Element(1), pl.Element(LANES), pl.Element(K)),
                               lambda d, j, src, valid: (d, src[j] * 8, 0))],
        out_specs=pl.BlockSpec((None, K, LANES), lambda d, j, src, valid: (d, 0, j)),
    )
    return pl.pallas_call(
        _w_in_relayout_kernel,
        grid_spec=grid_spec,
        out_shape=jax.ShapeDtypeStruct((depth, K, Z_DIM), BF16),
        compiler_params=_params(blocks, 0, ("parallel", "parallel")),
        name="w_in_relayout",
    )(jnp.asarray(src, jnp.int32), jnp.asarray(valid, jnp.int32), w_t)


def _prep_weights(w):
    depth = w["w_in"].shape[0]
    H, R = MLA_HEADS, MLA_KV_RANK
    w_in = _w_in_relayout(w["w_in"])

    wq = w["w_qb"].reshape(depth, MLA_Q_RANK, H, MLA_QK)
    half = MLA_ROPE // 2
    x1 = wq[..., MLA_NOPE:MLA_NOPE + half]
    x2 = wq[..., MLA_NOPE + half:]
    pad_q = jnp.zeros((depth, MLA_Q_RANK, H, HEAD_PAD - MLA_QK), F32)
    wq_p = jnp.concatenate([wq, pad_q], axis=-1)
    wq_sw = jnp.concatenate([jnp.zeros_like(wq[..., :MLA_NOPE]), -x2, x1, pad_q], axis=-1)
    wq2 = jnp.concatenate([wq_p.reshape(depth, MLA_Q_RANK, H * HEAD_PAD),
                           wq_sw.reshape(depth, MLA_Q_RANK, H * HEAD_PAD)], axis=-1).astype(BF16)

    wkv = w["w_kvb"].reshape(depth, R, H, MLA_NOPE + MLA_V)
    w_uk, w_uv = wkv[..., :MLA_NOPE], wkv[..., MLA_NOPE:]
    wk_top = jnp.concatenate([w_uk, jnp.zeros((depth, R, H, HEAD_PAD - MLA_NOPE), F32)], axis=-1)
    eye = jnp.eye(MLA_ROPE, dtype=F32)
    copy = jnp.concatenate([jnp.zeros((MLA_ROPE, MLA_NOPE), F32), eye,
                            jnp.zeros((MLA_ROPE, HEAD_PAD - MLA_QK), F32)], axis=-1)
    copy = jnp.broadcast_to(copy[:, None, :], (MLA_ROPE, H, HEAD_PAD))
    wk_bot = jnp.concatenate([copy, jnp.zeros((LANES - MLA_ROPE, H, HEAD_PAD), F32)], axis=0)
    wk = jnp.concatenate([wk_top, jnp.broadcast_to(wk_bot[None], (depth, LANES, H, HEAD_PAD))],
                         axis=1).reshape(depth, R + LANES, H * HEAD_PAD).astype(BF16)
    wv = w_uv.reshape(depth, R, H * MLA_V).astype(BF16)

    QC = R + LANES
    uk_t = jnp.transpose(w_uk, (0, 2, 3, 1))
    top = jnp.concatenate([uk_t, jnp.zeros((depth, H, MLA_NOPE, LANES), F32)], axis=-1)
    mid = jnp.concatenate([jnp.zeros((MLA_ROPE, R), F32), eye,
                           jnp.zeros((MLA_ROPE, LANES - MLA_ROPE), F32)], axis=-1)
    mid = jnp.broadcast_to(mid[None, None], (depth, H, MLA_ROPE, QC))
    bot = jnp.zeros((depth, H, HEAD_PAD - MLA_QK, QC), F32)
    wabs = jnp.concatenate([top, mid, bot], axis=2).astype(BF16)
    uv_h = jnp.transpose(w_uv, (0, 2, 1, 3)).reshape(depth, H // 2, 2, R, MLA_V)
    zed = jnp.zeros_like(uv_h[:, :, 0])
    wuv_bd = jnp.concatenate([jnp.concatenate([uv_h[:, :, 0], zed], axis=-1),
                              jnp.concatenate([zed, uv_h[:, :, 1]], axis=-1)], axis=2).astype(BF16)

    wg2 = jnp.concatenate([w["w_gla_g2"],
                           jnp.zeros((depth, LANES - GLA_GATE_RANK, GLA_HEADS * GLA_DK), F32)],
                          axis=1).astype(BF16)
    row = lambda a: a[:, None, :]
    out = dict(w_in=w_in, wq2=wq2, wk=wk, wv=wv, wabs=wabs, wuv_bd=wuv_bd, wg2=wg2,
               g_mix=row(w["g_mix"]), g_qa=row(w["g_qa"]), g_kva=row(w["g_kva"]),
               b_gla_g=row(w["b_gla_g"]), g_gla_o=row(w["g_gla_o"]), g_ffn=row(w["g_ffn"]),
               g_ple=row(w["g_ple"]), w_conv=w["w_conv"], w_ffn_conv=w["w_ffn_conv"])
    for name in ("w_br_conv", "w_br_mla", "w_br_gla", "w_o", "w_up", "w_down", "w_ple_gate", "w_ple"):
        out[name] = w[name].astype(BF16)
    return out


def _rope_tables(pos):
    half = MLA_ROPE // 2
    inv = ROPE_THETA ** (-jnp.arange(half, dtype=F32) / half)
    ang = pos.astype(F32)[:, None] * inv[None, :]
    cos, sin = jnp.cos(ang), jnp.sin(ang)
    n = pos.shape[0]
    ones = jnp.ones((n, MLA_NOPE), F32)
    zq = jnp.zeros((n, HEAD_PAD - MLA_QK), F32)
    zk = jnp.zeros((n, LANES - MLA_ROPE), F32)
    c_q = jnp.concatenate([ones, cos, cos, zq], axis=1)
    s_q = jnp.concatenate([jnp.zeros_like(ones), sin, sin, zq], axis=1)
    c_k = jnp.concatenate([cos, cos, zk], axis=1)
    s_k = jnp.concatenate([-sin, sin, zk], axis=1)
    return c_q, s_q, c_k, s_k


def _layer_tail(h, z, y_conv, y_mla, y_gla, W, layer):
    merged = _merge(y_conv, y_mla, y_gla, z, W, layer)
    return _matmul_res(merged, W["w_o"], layer, h, tm_pref=512, tn_pref=2048, name="out_proj")


def _prompt_trunk(x, p, W, g_final):
    B, T, D = x.shape
    depth = W["w_in"].shape[0]
    M = B * T
    h = x.reshape(M, D)
    p2 = p.reshape(depth, M, PLE_DIM)
    tabs = _rope_tables(jnp.arange(T, dtype=jnp.int32))
    rows, conv_st, gla_st, ffn_st = [], [], [], []
    for i in range(depth):
        z = _rms_matmul(h, W["g_mix"], W["w_in"], i, tm_pref=1024, tn_pref=1024, name="in_proj")
        y_conv, cst = _conv_prompt(z, W["w_conv"], i, B, T)
        q, mla_rows, k, v = _mla_prep(z, W, i, tabs, T, with_kv=True)
        y_mla = _mla_attn(q, k, v, B, T)
        y_gla, s_new = _gla_prompt(z, W, i, B, T)
        h = _layer_tail(h, z, y_conv, y_mla, y_gla, W, i)
        act, sg, sv = _ffn_up_prompt(h, W, i, B, T)
        h = _matmul_res(act, W["w_down"], i, h, tm_pref=512, tn_pref=1024, name="ffn_down")
        h = _ple(h, p2, W, i)
        rows.append(mla_rows.reshape(B, T, MLA_CACHE_DIM))
        conv_st.append(cst)
        gla_st.append(s_new)
        ffn_st.append(jnp.concatenate([sg, sv], axis=-1))
    y = _final_norm(h, g_final).reshape(B, T, D)
    return y, jnp.stack(rows), jnp.stack(conv_st), jnp.stack(gla_st), jnp.stack(ffn_st)


def _decode_trunk(x, p, W, g_final, cache, page_table, st_conv, st_gla, st_ffn):
    Bd, T, D = x.shape
    assert T == 1, "decode path handles one new token per request"
    depth = W["w_in"].shape[0]
    n_pages = page_table.shape[1]
    past_len = n_pages * cache.shape[2]
    h = x.reshape(Bd, D)
    p2 = p.reshape(depth, Bd, PLE_DIM)
    tabs = _rope_tables(jnp.full((Bd,), past_len, dtype=jnp.int32))
    pt_flat = page_table.reshape(-1)
    cache_t = jnp.swapaxes(cache, 2, 3)
    st_conv2 = st_conv.reshape(depth, Bd, 2 * CONV_DIM)
    st_ffn2 = st_ffn.reshape(depth, Bd, 4 * D_FF)
    rows, conv_st, gla_st, ffn_st = [], [], [], []
    for i in range(depth):
        z = _rms_matmul(h, W["g_mix"], W["w_in"], i, tm_pref=1024, tn_pref=1024, name="in_proj_decode")
        y_conv, cst = _conv_decode(z, st_conv2, W["w_conv"], i)
        q, mla_rows = _mla_prep(z, W, i, tabs, Bd, with_kv=False)
        qcat = _absorb(q, W["wabs"], i)
        o_lat = _dec_attn(pt_flat, qcat, mla_rows, cache_t, i, n_pages)
        y_mla = _dec_out(o_lat.reshape(Bd, MLA_HEADS * MLA_KV_RANK), W["wuv_bd"], i)
        y_gla, s_new = _gla_decode(z, st_gla, W, i)
        h = _layer_tail(h, z, y_conv, y_mla, y_gla, W, i)
        u = _rms_matmul(h, W["g_ffn"], W["w_up"], i, tm_pref=1024, tn_pref=1024, name="ffn_up_decode")
        act = _ffn_conv_decode(u, st_ffn2, W["w_ffn_conv"], i)
        h = _matmul_res(act, W["w_down"], i, h, tm_pref=512, tn_pref=1024, name="ffn_down_decode")
        h = _ple(h, p2, W, i)
        rows.append(mla_rows.reshape(Bd, 1, MLA_CACHE_DIM))
        conv_st.append(cst.reshape(Bd, 2, CONV_DIM))
        gla_st.append(s_new)
        ffn_st.append(jnp.stack([st_ffn[i, :, 1, :], u], axis=1))
    y = _final_norm(h, g_final).reshape(Bd, 1, D)
    return y, jnp.stack(rows), jnp.stack(conv_st), jnp.stack(gla_st), jnp.stack(ffn_st)


def kernel(x_prompt, x_sample, cache_mla, state_conv, state_gla, state_ffn, page_table, p_prompt, p_sample, g_mix, w_in, w_conv, g_qa, w_qb, g_kva, w_kvb, w_gla_g2, b_gla_g, g_gla_o, w_br_conv, w_br_mla, w_br_gla, w_o, g_ffn, w_up, w_ffn_conv, w_down, g_ple, w_ple_gate, w_ple, g_final):
    W = _prep_weights(dict(
        g_mix=g_mix, w_in=w_in, w_conv=w_conv, g_qa=g_qa, w_qb=w_qb, g_kva=g_kva, w_kvb=w_kvb,
        w_gla_g2=w_gla_g2, b_gla_g=b_gla_g, g_gla_o=g_gla_o, w_br_conv=w_br_conv,
        w_br_mla=w_br_mla, w_br_gla=w_br_gla, w_o=w_o, g_ffn=g_ffn, w_up=w_up,
        w_ffn_conv=w_ffn_conv, w_down=w_down, g_ple=g_ple, w_ple_gate=w_ple_gate, w_ple=w_ple))
    g_fin = g_final[None, :]
    y_p, mla_p, conv_p, gla_p, ffn_p = _prompt_trunk(x_prompt, p_prompt, W, g_fin)
    y_s, mla_s, conv_s, gla_s, ffn_s = _decode_trunk(
        x_sample, p_sample, W, g_fin, cache_mla, page_table, state_conv, state_gla, state_ffn)
    return (y_p, y_s, mla_p, mla_s, conv_p, conv_s, gla_p, gla_s, ffn_p, ffn_s)
```

```python
import functools

import jax
import jax.numpy as jnp
from jax import lax
from jax.experimental import pallas as pl
from jax.experimental.pallas import tpu as pltpu

F32 = jnp.float32
BF16 = jnp.bfloat16

D_MODEL = 2048
CONV_DIM = D_MODEL // 2
MLA_HEADS = 16
MLA_NOPE = 64
MLA_ROPE = 32
MLA_QK = MLA_NOPE + MLA_ROPE
MLA_V = 64
MLA_Q_RANK = D_MODEL // 4
MLA_KV_RANK = D_MODEL // 8
MLA_CACHE_DIM = MLA_KV_RANK + MLA_ROPE
ROPE_THETA = 10000.0
GLA_HEADS = 4
GLA_DK = D_MODEL // 16
GLA_DV = D_MODEL // 8
GLA_GATE_RANK = 16
GLA_TAU = 16.0
GLA_CHUNK = 64
GLA_LEVELS = (32, 16, 8)
GLA_DIAG = 8
D_FF = ((8 * D_MODEL // 3 + 255) // 256) * 256
PLE_DIM = 256
EPS = 1e-6

IN_SPLITS = (CONV_DIM, CONV_DIM, CONV_DIM, MLA_Q_RANK, MLA_CACHE_DIM,
             GLA_HEADS * GLA_DK, GLA_HEADS * GLA_DK, GLA_HEADS * GLA_DV, GLA_GATE_RANK,
             GLA_HEADS * GLA_DV, D_MODEL, D_MODEL, D_MODEL)

LANES = 128
HEAD_PAD = LANES
VMEM_CAP = 60 * 1024 * 1024
HALO = 16
ROW_SUB = 256
COL_SUB = 256
RELAYOUT_GROUP = 4

Z_CB, Z_CC, Z_CH = 0, CONV_DIM, 2 * CONV_DIM
Z_GA = 3 * CONV_DIM
Z_GV = Z_GA + 3 * D_MODEL
Z_GR = Z_GV + GLA_HEADS * GLA_DV
Z_QA = Z_GR + GLA_HEADS * GLA_DV
Z_GQ = Z_QA + MLA_Q_RANK
Z_GK = Z_GQ + GLA_HEADS * GLA_DK
Z_KVG = Z_GK + GLA_HEADS * GLA_DK
KVG_W = 4 * LANES
Z_GG = Z_KVG + 3 * LANES
Z_DIM = Z_KVG + KVG_W


def _pick(n, pref, mult=8):
    if n <= pref:
        return n
    for t in range(pref, 0, -1):
        if n % t == 0 and t % mult == 0:
            return t
    raise ValueError(f"no tile for {n}")


def _params(block_bytes, scratch_bytes=0, sem=None):
    need = 2 * block_bytes + scratch_bytes + (8 << 20)
    kw = dict(vmem_limit_bytes=int(min(max(need, 32 << 20), VMEM_CAP)))
    if sem is not None:
        kw["dimension_semantics"] = sem
    return pltpu.CompilerParams(**kw)


def _nbytes(shape, dtype):
    n = 1
    for s in shape:
        n *= s
    return n * jnp.dtype(dtype).itemsize


def _rmsn(x, g):
    y = x * lax.rsqrt(jnp.mean(x * x, axis=-1, keepdims=True) + EPS)
    return y * g


def _dot(a, b):
    return jnp.dot(a, b, preferred_element_type=F32)


def _dot_nt(a, b):
    return lax.dot_general(a, b, (((1,), (1,)), ((), ())), preferred_element_type=F32)


def _dot_tn(a, b):
    return lax.dot_general(a, b, (((0,), (0,)), ((), ())), preferred_element_type=F32)


def _rms_mm_kernel(x_ref, g_ref, w_ref, o_ref, xn_ref):
    @pl.when(pl.program_id(1) == 0)
    def _():
        xn_ref[...] = _rmsn(x_ref[...], g_ref[...]).astype(BF16)

    o_ref[...] = _dot(xn_ref[...], w_ref[...])


def _rms_matmul(x, g, w, layer, *, tm_pref, tn_pref, name):
    M, K = x.shape
    N = w.shape[-1]
    tm = _pick(M, tm_pref)
    tn = _pick(N, tn_pref, LANES)
    blocks = _nbytes((tm, K), F32) + _nbytes((K, tn), BF16) + _nbytes((tm, tn), F32)
    return pl.pallas_call(
        _rms_mm_kernel,
        grid=(M // tm, N // tn),
        in_specs=[pl.BlockSpec((tm, K), lambda i, j: (i, 0)),
                  pl.BlockSpec((None, 1, K), lambda i, j: (layer, 0, 0)),
                  pl.BlockSpec((None, K, tn), lambda i, j: (layer, 0, j))],
        out_specs=pl.BlockSpec((tm, tn), lambda i, j: (i, j)),
        out_shape=jax.ShapeDtypeStruct((M, N), F32),
        scratch_shapes=[pltpu.VMEM((tm, K), BF16)],
        compiler_params=_params(blocks, _nbytes((tm, K), BF16), ("parallel", "arbitrary")),
        name=name,
    )(x, g, w)


def _mm_res_kernel(x_ref, w_ref, r_ref, o_ref):
    o_ref[...] = r_ref[...] + _dot(x_ref[...].astype(BF16), w_ref[...])


def _matmul_res(x, w, layer, res, *, tm_pref, tn_pref, name):
    M, K = x.shape
    N = w.shape[-1]
    tm = _pick(M, tm_pref, 16)
    tn = _pick(N, tn_pref, LANES)
    blocks = _nbytes((tm, K), x.dtype) + _nbytes((K, tn), BF16) + 2 * _nbytes((tm, tn), F32)
    return pl.pallas_call(
        _mm_res_kernel,
        grid=(N // tn, M // tm),
        in_specs=[pl.BlockSpec((tm, K), lambda j, i: (i, 0)),
                  pl.BlockSpec((None, K, tn), lambda j, i: (layer, 0, j)),
                  pl.BlockSpec((tm, tn), lambda j, i: (i, j))],
        out_specs=pl.BlockSpec((tm, tn), lambda j, i: (i, j)),
        out_shape=jax.ShapeDtypeStruct((M, N), F32),
        compiler_params=_params(blocks, 0, ("parallel", "parallel")),
        name=name,
    )(x, w, res)


def _shift_rows(x, k):
    rows = lax.broadcasted_iota(jnp.int32, x.shape, 0)
    return jnp.where(rows >= k, pltpu.roll(x, k, axis=0), 0.0)


def _conv_prompt_kernel(cb_ref, cc_ref, ch_ref, w_ref, y_ref, st_ref):
    T = cb_ref.shape[0]
    x = cc_ref[...] * ch_ref[...]
    zc = _shift_rows(x, 2) * w_ref[0:1, :] + _shift_rows(x, 1) * w_ref[1:2, :] + x * w_ref[2:3, :]
    y_ref[...] = (cb_ref[...] * zc).astype(y_ref.dtype)
    st_ref[...] = x[T - 2:T, :]


def _conv_prompt(z, w_conv, layer, B, T):
    tc = 2 * LANES
    nc = CONV_DIM // tc
    blocks = 3 * _nbytes((T, tc), F32) + _nbytes((T, tc), BF16)
    return pl.pallas_call(
        _conv_prompt_kernel,
        grid=(B, nc),
        in_specs=[pl.BlockSpec((T, tc), lambda b, c: (b, Z_CB // tc + c)),
                  pl.BlockSpec((T, tc), lambda b, c: (b, Z_CC // tc + c)),
                  pl.BlockSpec((T, tc), lambda b, c: (b, Z_CH // tc + c)),
                  pl.BlockSpec((None, 3, tc), lambda b, c: (layer, 0, c))],
        out_specs=[pl.BlockSpec((T, tc), lambda b, c: (b, c)),
                   pl.BlockSpec((None, 2, tc), lambda b, c: (b, 0, c))],
        out_shape=[jax.ShapeDtypeStruct((B * T, CONV_DIM), BF16),
                   jax.ShapeDtypeStruct((B, 2, CONV_DIM), F32)],
        compiler_params=_params(blocks, 0, ("parallel", "parallel")),
        name="conv_prompt",
    )(z, z, z, w_conv)


def _conv_decode_kernel(cb_ref, cc_ref, ch_ref, st_ref, w_ref, y_ref, nst_ref):
    C = cb_ref.shape[1]
    x = cc_ref[...] * ch_ref[...]
    s0 = st_ref[:, :C]
    s1 = st_ref[:, C:]
    zc = s0 * w_ref[0:1, :] + s1 * w_ref[1:2, :] + x * w_ref[2:3, :]
    y_ref[...] = (cb_ref[...] * zc).astype(y_ref.dtype)
    nst_ref[:, :C] = s1
    nst_ref[:, C:] = x


def _conv_decode(z, state2d, w_conv, layer):
    Bd = z.shape[0]
    C = CONV_DIM
    blocks = 3 * _nbytes((Bd, C), F32) + 4 * _nbytes((Bd, C), F32) + _nbytes((Bd, C), BF16)
    return pl.pallas_call(
        _conv_decode_kernel,
        grid=(1,),
        in_specs=[pl.BlockSpec((Bd, C), lambda i: (0, Z_CB // C)),
                  pl.BlockSpec((Bd, C), lambda i: (0, Z_CC // C)),
                  pl.BlockSpec((Bd, C), lambda i: (0, Z_CH // C)),
                  pl.BlockSpec((None, Bd, 2 * C), lambda i: (layer, 0, 0)),
                  pl.BlockSpec((None, 3, C), lambda i: (layer, 0, 0))],
        out_specs=[pl.BlockSpec((Bd, C), lambda i: (0, 0)),
                   pl.BlockSpec((Bd, 2 * C), lambda i: (0, 0))],
        out_shape=[jax.ShapeDtypeStruct((Bd, C), BF16),
                   jax.ShapeDtypeStruct((Bd, 2 * C), F32)],
        compiler_params=_params(blocks),
        name="conv_decode",
    )(z, z, z, state2d, w_conv)


def _mla_prep_kernel(qa_ref, kvg_ref, gqa_ref, gkv_ref, wq_ref, wk_ref, wv_ref,
                     cq_ref, sq_ref, ck_ref, sk_ref, *out_refs, with_kv):
    q_ref, rows_ref = out_refs[0], out_refs[1]
    HP = MLA_HEADS * HEAD_PAD
    cq = _rmsn(qa_ref[...], gqa_ref[...]).astype(BF16)
    qq = _dot(cq, wq_ref[...])
    c_q, s_q = cq_ref[...], sq_ref[...]
    for h in range(MLA_HEADS):
        lo = h * HEAD_PAD
        q_ref[:, lo:lo + HEAD_PAD] = (qq[:, lo:lo + HEAD_PAD] * c_q
                                      + qq[:, HP + lo:HP + lo + HEAD_PAD] * s_q).astype(q_ref.dtype)
    ckv = _rmsn(kvg_ref[:, :MLA_KV_RANK], gkv_ref[...])
    kr = kvg_ref[:, MLA_KV_RANK:MLA_KV_RANK + LANES]
    lane = lax.broadcasted_iota(jnp.int32, kr.shape, 1)
    half = MLA_ROPE // 2
    partner = jnp.where(lane < half, pltpu.roll(kr, LANES - half, axis=1), pltpu.roll(kr, half, axis=1))
    kr = kr * ck_ref[...] + partner * sk_ref[...]
    rows_ref[:, :MLA_KV_RANK] = ckv
    rows_ref[:, MLA_KV_RANK:] = kr[:, :MLA_ROPE]
    if with_kv:
        k_ref, v_ref = out_refs[2], out_refs[3]
        ckv_b = ckv.astype(BF16)
        kin = jnp.concatenate([ckv_b, kr.astype(BF16)], axis=1)
        k_ref[...] = _dot(kin, wk_ref[...]).astype(k_ref.dtype)
        v_ref[...] = _dot(ckv_b, wv_ref[...]).astype(v_ref.dtype)


def _mla_prep(z, W, layer, tabs, T, *, with_kv):
    M = z.shape[0]
    tm = _pick(min(M, T), 512, 16)
    nt = max(T // tm, 1)
    HP = MLA_HEADS * HEAD_PAD
    HV = MLA_HEADS * MLA_V
    KIN = MLA_KV_RANK + LANES
    tab_spec = pl.BlockSpec((tm, LANES), lambda i: (i % nt, 0))
    out_specs = [pl.BlockSpec((tm, HP), lambda i: (i, 0)),
                 pl.BlockSpec((tm, MLA_CACHE_DIM), lambda i: (i, 0))]
    out_shape = [jax.ShapeDtypeStruct((M, HP), BF16),
                 jax.ShapeDtypeStruct((M, MLA_CACHE_DIM), F32)]
    if with_kv:
        out_specs += [pl.BlockSpec((tm, HP), lambda i: (i, 0)),
                      pl.BlockSpec((tm, HV), lambda i: (i, 0))]
        out_shape += [jax.ShapeDtypeStruct((M, HP), BF16),
                      jax.ShapeDtypeStruct((M, HV), BF16)]
    blocks = (_nbytes((tm, MLA_Q_RANK + KVG_W), F32) + _nbytes((MLA_Q_RANK, 2 * HP), BF16)
              + _nbytes((KIN, HP), BF16) + _nbytes((MLA_KV_RANK, HV), BF16)
              + 4 * _nbytes((tm, LANES), F32) + _nbytes((tm, 2 * HP + HV), BF16)
              + _nbytes((tm, 3 * LANES), F32) + _nbytes((tm, 2 * HP), F32))
    return pl.pallas_call(
        functools.partial(_mla_prep_kernel, with_kv=with_kv),
        grid=(M // tm,),
        in_specs=[pl.BlockSpec((tm, MLA_Q_RANK), lambda i: (i, Z_QA // MLA_Q_RANK)),
                  pl.BlockSpec((tm, KVG_W), lambda i: (i, Z_KVG // KVG_W)),
                  pl.BlockSpec((None, 1, MLA_Q_RANK), lambda i: (layer, 0, 0)),
                  pl.BlockSpec((None, 1, MLA_KV_RANK), lambda i: (layer, 0, 0)),
                  pl.BlockSpec((None, MLA_Q_RANK, 2 * HP), lambda i: (layer, 0, 0)),
                  pl.BlockSpec((None, KIN, HP), lambda i: (layer, 0, 0)),
                  pl.BlockSpec((None, MLA_KV_RANK, HV), lambda i: (layer, 0, 0)),
                  tab_spec, tab_spec, tab_spec, tab_spec],
        out_specs=out_specs,
        out_shape=out_shape,
        compiler_params=_params(blocks, 0, ("parallel",)),
        name="mla_prep" if with_kv else "mla_prep_decode",
    )(z, z, W["g_qa"], W["g_kva"], W["wq2"], W["wk"], W["wv"], *tabs)


def _mla_attn_kernel(q_ref, k_ref, v_ref, o_ref, *, tq, tk):
    qi = pl.program_id(2)
    nr = tq // tk
    c = (MLA_QK ** -0.5) * 1.4426950408889634
    vlane = lax.broadcasted_iota(jnp.int32, (tk, LANES), 1)
    olane = lax.broadcasted_iota(jnp.int32, (tk, LANES), 1)
    tri = lax.broadcasted_iota(jnp.int32, (tk, tk), 1) <= lax.broadcasted_iota(jnp.int32, (tk, tk), 0)

    def scores(kb, first_row):
        ks = pl.multiple_of(kb * tk, tk)
        rows = slice(first_row * tk, tq)
        return (_dot_nt(q_ref[rows, :HEAD_PAD], k_ref[pl.ds(ks, tk), :HEAD_PAD]),
                _dot_nt(q_ref[rows, HEAD_PAD:], k_ref[pl.ds(ks, tk), HEAD_PAD:]))

    def head(s_all, vt, ms, accs, first_row, masked_row):
        new_ms, alphas, ps = [], [], []
        for i, r in enumerate(range(first_row, nr)):
            s = s_all[i * tk:(i + 1) * tk]
            if r == masked_row:
                s = jnp.where(tri, s, -jnp.inf)
            m_new = jnp.maximum(ms[i], jnp.max(s, axis=-1, keepdims=True))
            alphas.append(jnp.exp2((ms[i] - m_new) * c))
            ps.append(jnp.exp2((s - m_new) * c).astype(BF16))
            new_ms.append(m_new)
        pv = _dot(ps[0] if len(ps) == 1 else jnp.concatenate(ps, axis=0), vt)
        new_accs = [alphas[i] * accs[i] + pv[i * tk:(i + 1) * tk] for i in range(len(ps))]
        return new_ms, new_accs

    def block(kb, state, s_both, first_row, masked_row):
        ks = pl.multiple_of(kb * tk, tk)
        v = v_ref[pl.ds(ks, tk), :].astype(F32)
        v0 = jnp.where(vlane < MLA_V, v, jnp.where(vlane == MLA_V, 1.0, 0.0)).astype(BF16)
        v1 = jnp.where(vlane >= MLA_V, v, jnp.where(vlane == 0, 1.0, 0.0)).astype(BF16)
        live = state[first_row:]
        m0, a0 = head(s_both[0], v0, [t[0] for t in live], [t[1] for t in live], first_row, masked_row)
        m1, a1 = head(s_both[1], v1, [t[2] for t in live], [t[3] for t in live], first_row, masked_row)
        return state[:first_row] + tuple(zip(m0, a0, m1, a1))

    neg = jnp.full((tk, 1), -jnp.inf, F32)
    nil = jnp.zeros((tk, LANES), F32)
    state = tuple((neg, nil, neg, nil) for _ in range(nr))
    base = qi * nr

    def full_blocks(i, st):
        for u in range(nr):
            st = block(i * nr + u, st, scores(i * nr + u, 0), 0, -1)
        return st

    state = lax.fori_loop(0, qi, full_blocks, state)
    for d in range(nr):
        state = block(base + d, state, scores(base + d, d), d, d)
    for r in range(nr):
        m0, a0, m1, a1 = state[r]
        out = jnp.where(olane < MLA_V, a0 / a0[:, MLA_V:MLA_V + 1], a1 / a1[:, 0:1])
        o_ref[r * tk:(r + 1) * tk, :] = out.astype(o_ref.dtype)


def _mla_attn(q, k, v, B, T):
    tq = _pick(T, 1024, 16)
    tk = _pick(T, 256, 16)
    assert tq % tk == 0
    nq = T // tq
    blocks = (_nbytes((tq, 2 * HEAD_PAD), BF16) + _nbytes((T, 2 * HEAD_PAD), BF16)
              + _nbytes((T, LANES), BF16) + _nbytes((tq, LANES), BF16))
    return pl.pallas_call(
        functools.partial(_mla_attn_kernel, tq=tq, tk=tk),
        grid=(B, MLA_HEADS // 2, nq),
        in_specs=[pl.BlockSpec((tq, 2 * HEAD_PAD), lambda b, hp, i: (b * nq + i, hp)),
                  pl.BlockSpec((T, 2 * HEAD_PAD), lambda b, hp, i: (b, hp)),
                  pl.BlockSpec((T, LANES), lambda b, hp, i: (b, hp))],
        out_specs=pl.BlockSpec((tq, LANES), lambda b, hp, i: (b * nq + i, hp)),
        out_shape=jax.ShapeDtypeStruct((B * T, MLA_HEADS * MLA_V), BF16),
        compiler_params=_params(blocks, 8 << 20, ("parallel", "parallel", "arbitrary")),
        name="mla_attn",
    )(q, k, v)


def _absorb_kernel(q_ref, w_ref, o_ref):
    o_ref[...] = _dot(q_ref[...], w_ref[...]).astype(o_ref.dtype)


def _absorb(q, wabs, layer):
    Bd = q.shape[0]
    QC = wabs.shape[-1]
    return pl.pallas_call(
        _absorb_kernel,
        grid=(MLA_HEADS,),
        in_specs=[pl.BlockSpec((Bd, HEAD_PAD), lambda h: (0, h)),
                  pl.BlockSpec((None, None, HEAD_PAD, QC), lambda h: (layer, h, 0, 0))],
        out_specs=pl.BlockSpec((Bd, QC), lambda h: (0, h)),
        out_shape=jax.ShapeDtypeStruct((Bd, MLA_HEADS * QC), BF16),
        name="mla_absorb",
    )(q, wabs)


def _page_copy(cache_ref, buf_ref, sem_ref, layer, page, slot, j, page_size):
    return pltpu.make_async_copy(cache_ref.at[layer, page],
                                 buf_ref.at[slot, :, pl.ds(j * page_size, page_size)],
                                 sem_ref.at[slot])


def _dec_attn_kernel(pt_ref, q_ref, new_ref, cache_ref, o_ref, buf_ref, kb_ref, sem_ref,
                     *, layer, n_pages, page_size, chunk):
    b = pl.program_id(0)
    nb = pl.num_programs(0)
    scale = MLA_QK ** -0.5
    R = MLA_KV_RANK

    def fetch(bb, slot):
        for j in range(n_pages):
            _page_copy(cache_ref, buf_ref, sem_ref, layer, pt_ref[bb * n_pages + j], slot, j,
                       page_size).start()

    @pl.when(b == 0)
    def _():
        fetch(0, 0)

    @pl.when(b + 1 < nb)
    def _():
        fetch(b + 1, (b + 1) % 2)

    slot = b % 2
    for j in range(n_pages):
        _page_copy(cache_ref, buf_ref, sem_ref, layer, 0, slot, j, page_size).wait()

    q = q_ref[0]
    q_lat = q[:, :R]
    q_rope = q[:, R:R + MLA_ROPE]
    n_chunks = (n_pages * page_size) // chunk
    parts = []
    for c in range(n_chunks):
        cols = slice(c * chunk, (c + 1) * chunk)
        rows = buf_ref[slot, :, cols].astype(BF16)
        kb_ref[:, cols] = rows[:R]
        parts.append(_dot(q_lat, rows[:R]) + _dot(q_rope, rows[R:]))
    s = jnp.concatenate(parts, axis=1)
    new = new_ref[0].astype(BF16).astype(F32)
    s_new = jnp.sum(q.astype(F32)[:, :MLA_CACHE_DIM] * new, axis=-1, keepdims=True)
    m = jnp.maximum(jnp.max(s, axis=-1, keepdims=True), s_new)
    p = jnp.exp((s - m) * scale)
    p_new = jnp.exp((s_new - m) * scale)
    l = jnp.sum(p, axis=-1, keepdims=True) + p_new
    acc = p_new * new[:, :R]
    pb = p.astype(BF16)
    for c in range(n_chunks):
        cols = slice(c * chunk, (c + 1) * chunk)
        acc = acc + _dot_nt(pb[:, cols], kb_ref[:, cols])
    o_ref[0] = acc / l


def _dec_attn(pt_flat, qcat, rows_new, cache_t, layer, n_pages):
    Bd = rows_new.shape[0]
    page_size = cache_t.shape[3]
    L = n_pages * page_size
    QC = qcat.shape[-1] // MLA_HEADS
    chunk = _pick(L, 1024, page_size)
    q3 = qcat.reshape(Bd, MLA_HEADS, QC)
    new3 = rows_new.reshape(Bd, 1, MLA_CACHE_DIM)
    scratch = 2 * _nbytes((MLA_CACHE_DIM, L), F32) + _nbytes((MLA_KV_RANK, L), BF16)
    grid_spec = pltpu.PrefetchScalarGridSpec(
        num_scalar_prefetch=1,
        grid=(Bd,),
        in_specs=[pl.BlockSpec((1, MLA_HEADS, QC), lambda b, pt: (b, 0, 0)),
                  pl.BlockSpec((1, 1, MLA_CACHE_DIM), lambda b, pt: (b, 0, 0)),
                  pl.BlockSpec(memory_space=pl.ANY)],
        out_specs=pl.BlockSpec((1, MLA_HEADS, MLA_KV_RANK), lambda b, pt: (b, 0, 0)),
        scratch_shapes=[pltpu.VMEM((2, MLA_CACHE_DIM, L), F32),
                        pltpu.VMEM((MLA_KV_RANK, L), BF16),
                        pltpu.SemaphoreType.DMA((2,))],
    )
    return pl.pallas_call(
        functools.partial(_dec_attn_kernel, layer=layer, n_pages=n_pages, page_size=page_size,
                          chunk=chunk),
        grid_spec=grid_spec,
        out_shape=jax.ShapeDtypeStruct((Bd, MLA_HEADS, MLA_KV_RANK), F32),
        compiler_params=_params(1 << 20, scratch, ("arbitrary",)),
        name="mla_decode_attn",
    )(pt_flat, q3, new3, cache_t)


def _dec_out_kernel(o_ref, w_ref, y_ref):
    y_ref[...] = _dot(o_ref[...].astype(BF16), w_ref[...]).astype(y_ref.dtype)


def _dec_out(o_lat2d, wuv_bd, layer):
    Bd = o_lat2d.shape[0]
    return pl.pallas_call(
        _dec_out_kernel,
        grid=(MLA_HEADS // 2,),
        in_specs=[pl.BlockSpec((Bd, 2 * MLA_KV_RANK), lambda hp: (0, hp)),
                  pl.BlockSpec((None, None, 2 * MLA_KV_RANK, 2 * MLA_V), lambda hp: (layer, hp, 0, 0))],
        out_specs=pl.BlockSpec((Bd, 2 * MLA_V), lambda hp: (0, hp)),
        out_shape=jax.ShapeDtypeStruct((Bd, MLA_HEADS * MLA_V), BF16),
        name="mla_decode_out",
    )(o_lat2d, wuv_bd)


def _log_decay(gg, wg, bg):
    g_pre = _dot(gg.astype(BF16), wg) + bg
    return (jnp.minimum(g_pre, 0.0) - jnp.log1p(jnp.exp(-jnp.abs(g_pre)))) / GLA_TAU


def _head_out(o, g_o, gr):
    return _rmsn(o, g_o) * (gr * jax.nn.sigmoid(gr))


def _gla_masks():
    C = GLA_CHUNK
    row = lax.broadcasted_iota(jnp.int32, (C, C), 0)
    col = lax.broadcasted_iota(jnp.int32, (C, C), 1)
    level = jnp.zeros((C, C), jnp.int32)
    for n, size in enumerate(GLA_LEVELS):
        sh = size.bit_length() - 1
        rb, cb = row >> sh, col >> sh
        level = jnp.where(rb - cb == 1, jnp.where((rb & 1) == 1, n + 1, level), level)
    rel = col - ((row >> 3) << 3)
    return level, rel, row >= col


def _gla_scores(q, k, b, level, rel, causal):
    C, DK = q.shape
    a = jnp.zeros((C, C), F32)
    for n, size in enumerate(GLA_LEVELS):
        ref = jnp.concatenate(
            [jnp.broadcast_to(b[p + size - 1:p + size, :], (2 * size, DK))
             for p in range(0, C, 2 * size)], axis=0)
        e = jnp.exp(-jnp.abs(b - ref))
        a = jnp.where(level == n + 1, _dot_nt((q * e).astype(BF16), (k * e).astype(BF16)), a)
    G = C // GLA_DIAG
    k3 = k.reshape(G, GLA_DIAG, DK)
    b3 = b.reshape(G, GLA_DIAG, DK)
    for j in range(GLA_DIAG):
        kj = jnp.broadcast_to(k3[:, j:j + 1, :], (G, GLA_DIAG, DK)).reshape(C, DK)
        bj = jnp.broadcast_to(b3[:, j:j + 1, :], (G, GLA_DIAG, DK)).reshape(C, DK)
        w = q * kj * jnp.exp(b - bj)
        a = jnp.where(rel == j, jnp.sum(w, axis=-1, keepdims=True), a)
    return jnp.where(causal, a, 0.0)


def _gla_prompt_kernel(gq_ref, gk_ref, gv_ref, gr_ref, gg_ref, wg_ref, bg_ref, go_ref,
                       y_ref, s_ref, la_ref):
    T = gq_ref.shape[0]
    C = GLA_CHUNK
    la_ref[...] = _log_decay(gg_ref[...], wg_ref[...], bg_ref[...])
    level, rel, causal = _gla_masks()
    tri = causal.astype(F32)
    g_o = go_ref[...]

    def chunk(c, st):
        r0 = pl.multiple_of(c * C, C)
        q = gq_ref[pl.ds(r0, C), :] * (GLA_DK ** -0.5)
        k = gk_ref[pl.ds(r0, C), :]
        v = gv_ref[pl.ds(r0, C), :].astype(BF16)
        b = jnp.dot(tri, la_ref[pl.ds(r0, C), :], precision=lax.Precision.HIGHEST,
                    preferred_element_type=F32)
        b_last = b[C - 1:C, :]
        a = _gla_scores(q, k, b, level, rel, causal)
        o = _dot_nt((q * jnp.exp(b)).astype(BF16), st.astype(BF16)) + _dot(a.astype(BF16), v)
        y_ref[pl.ds(r0, C), :] = _head_out(o, g_o, gr_ref[pl.ds(r0, C), :]).astype(y_ref.dtype)
        kd = (k * jnp.exp(b_last - b)).astype(BF16)
        return st * jnp.exp(b_last) + _dot_tn(v, kd)

    st = lax.fori_loop(0, T // C, chunk, jnp.zeros((GLA_DV, GLA_DK), F32), unroll=8)
    s_ref[...] = st.T


def _gla_prompt(z, W, layer, B, T):
    DK, DV, H = GLA_DK, GLA_DV, GLA_HEADS
    blocks = (3 * _nbytes((T, DK), F32) + 2 * _nbytes((T, DV), F32) + _nbytes((T, DV), BF16)
              + _nbytes((DK, DV), F32))
    scratch = _nbytes((T, DK), F32)
    return pl.pallas_call(
        _gla_prompt_kernel,
        grid=(B, H),
        in_specs=[pl.BlockSpec((T, DK), lambda b, h: (b, Z_GQ // DK + h)),
                  pl.BlockSpec((T, DK), lambda b, h: (b, Z_GK // DK + h)),
                  pl.BlockSpec((T, DV), lambda b, h: (b, Z_GV // DV + h)),
                  pl.BlockSpec((T, DV), lambda b, h: (b, Z_GR // DV + h)),
                  pl.BlockSpec((T, LANES), lambda b, h: (b, Z_GG // LANES)),
                  pl.BlockSpec((None, LANES, DK), lambda b, h: (layer, 0, h)),
                  pl.BlockSpec((None, 1, DK), lambda b, h: (layer, 0, h)),
                  pl.BlockSpec((None, 1, DV), lambda b, h: (layer, 0, 0))],
        out_specs=[pl.BlockSpec((T, DV), lambda b, h: (b, h)),
                   pl.BlockSpec((None, None, DK, DV), lambda b, h: (b, h, 0, 0))],
        out_shape=[jax.ShapeDtypeStruct((B * T, H * DV), BF16),
                   jax.ShapeDtypeStruct((B, H, DK, DV), F32)],
        scratch_shapes=[pltpu.VMEM((T, DK), F32)],
        compiler_params=_params(blocks, scratch, ("parallel", "parallel")),
        name="gla_prompt",
    )(z, z, z, z, z, W["wg2"], W["b_gla_g"], W["g_gla_o"])


def _gla_decode_kernel(gq_ref, gk_ref, gv_ref, gr_ref, gg_ref, s_ref, wg_ref, bg_ref, go_ref,
                       y_ref, ns_ref, o_scr):
    nb = gq_ref.shape[0]
    DK, DV, H = GLA_DK, GLA_DV, GLA_HEADS
    la = _log_decay(gg_ref[...], wg_ref[...], bg_ref[...])
    a = jnp.exp(la)
    q = gq_ref[...] * (GLA_DK ** -0.5)
    k = gk_ref[...]
    v = gv_ref[...]
    qa = q * a
    pieces = [x[:, h * DK:(h + 1) * DK] for x in (a, k, qa) for h in range(H)]
    pad = LANES - 3 * H * nb
    stack = jnp.concatenate(pieces + [jnp.zeros((pad, DK), F32)], axis=0)
    cols = stack.T
    qk = q * k
    for h in range(H):
        att = jnp.sum(qk[:, h * DK:(h + 1) * DK], axis=-1, keepdims=True)
        o_scr[:, h * DV:(h + 1) * DV] = att * v[:, h * DV:(h + 1) * DV]
    for i in range(nb):
        for h in range(H):
            c = h * nb + i
            a_col = cols[:, c:c + 1]
            k_col = cols[:, H * nb + c:H * nb + c + 1]
            qa_col = cols[:, 2 * H * nb + c:2 * H * nb + c + 1]
            s_old = s_ref[i, h]
            v_row = v[i:i + 1, h * DV:(h + 1) * DV]
            ns_ref[i, h] = a_col * s_old + k_col * v_row
            o_scr[i:i + 1, h * DV:(h + 1) * DV] += jnp.sum(qa_col * s_old, axis=0, keepdims=True)
    g_o = go_ref[...]
    gr = gr_ref[...]
    for h in range(H):
        sl = slice(h * DV, (h + 1) * DV)
        y_ref[:, sl] = _head_out(o_scr[:, sl], g_o, gr[:, sl]).astype(y_ref.dtype)


def _gla_decode(z, state, W, layer):
    Bd = z.shape[0]
    DK, DV, H = GLA_DK, GLA_DV, GLA_HEADS
    nb = 8
    HK, HV = H * DK, H * DV
    blocks = (2 * _nbytes((nb, HK), F32) + 3 * _nbytes((nb, HV), F32) + _nbytes((nb, LANES), F32)
              + 2 * _nbytes((nb, H, DK, DV), F32) + _nbytes((LANES, HK), BF16))
    return pl.pallas_call(
        _gla_decode_kernel,
        grid=(Bd // nb,),
        in_specs=[pl.BlockSpec((nb, HK), lambda i: (i, Z_GQ // HK)),
                  pl.BlockSpec((nb, HK), lambda i: (i, Z_GK // HK)),
                  pl.BlockSpec((nb, HV), lambda i: (i, Z_GV // HV)),
                  pl.BlockSpec((nb, HV), lambda i: (i, Z_GR // HV)),
                  pl.BlockSpec((nb, LANES), lambda i: (i, Z_GG // LANES)),
                  pl.BlockSpec((None, nb, H, DK, DV), lambda i: (layer, i, 0, 0, 0)),
                  pl.BlockSpec((None, LANES, HK), lambda i: (layer, 0, 0)),
                  pl.BlockSpec((None, 1, HK), lambda i: (layer, 0, 0)),
                  pl.BlockSpec((None, 1, DV), lambda i: (layer, 0, 0))],
        out_specs=[pl.BlockSpec((nb, HV), lambda i: (i, 0)),
                   pl.BlockSpec((nb, H, DK, DV), lambda i: (i, 0, 0, 0))],
        out_shape=[jax.ShapeDtypeStruct((Bd, HV), F32),
                   jax.ShapeDtypeStruct((Bd, H, DK, DV), F32)],
        scratch_shapes=[pltpu.VMEM((nb, HV), F32)],
        compiler_params=_params(blocks, 0, ("parallel",)),
        name="gla_decode",
    )(z, z, z, z, z, state, W["wg2"], W["b_gla_g"], W["g_gla_o"])


def _merge_kernel(yc_ref, ym_ref, yg_ref, gc_ref, gm_ref, gg_ref, wc_ref, wm_ref, wg_ref, o_ref):
    tm = o_ref.shape[0]
    rs = min(ROW_SUB, tm)
    for r in range(0, tm, rs):
        rows = slice(r, r + rs)
        acc = jax.nn.sigmoid(gc_ref[rows, :]) * _dot(yc_ref[rows, :].astype(BF16), wc_ref[...])
        acc = acc + jax.nn.sigmoid(gm_ref[rows, :]) * _dot(ym_ref[rows, :].astype(BF16), wm_ref[...])
        acc = acc + jax.nn.sigmoid(gg_ref[rows, :]) * _dot(yg_ref[rows, :].astype(BF16), wg_ref[...])
        o_ref[rows, :] = acc.astype(o_ref.dtype)


def _merge(y_conv, y_mla, y_gla, z, W, layer):
    M = z.shape[0]
    tm = _pick(M, 1024, 16)
    tn = 512
    C = y_conv.shape[1]
    ga = Z_GA // tn
    gstep = D_MODEL // tn
    blocks = (3 * _nbytes((tm, C), F32) + 3 * _nbytes((tm, tn), F32) + 3 * _nbytes((C, tn), BF16)
              + _nbytes((tm, tn), BF16))
    y_spec = pl.BlockSpec((tm, C), lambda i, j: (i, 0))
    w_spec = pl.BlockSpec((None, C, tn), lambda i, j: (layer, 0, j))
    return pl.pallas_call(
        _merge_kernel,
        grid=(M // tm, D_MODEL // tn),
        in_specs=[y_spec, y_spec, y_spec,
                  pl.BlockSpec((tm, tn), lambda i, j: (i, ga + j)),
                  pl.BlockSpec((tm, tn), lambda i, j: (i, ga + gstep + j)),
                  pl.BlockSpec((tm, tn), lambda i, j: (i, ga + 2 * gstep + j)),
                  w_spec, w_spec, w_spec],
        out_specs=pl.BlockSpec((tm, tn), lambda i, j: (i, j)),
        out_shape=jax.ShapeDtypeStruct((M, D_MODEL), BF16),
        compiler_params=_params(blocks, 0, ("parallel", "arbitrary")),
        name="merge",
    )(y_conv, y_mla, y_gla, z, z, z, W["w_br_conv"], W["w_br_mla"], W["w_br_gla"])


def _ffn_up_kernel(h_ref, halo_ref, g_ref, wg_ref, wv_ref, cg_ref, cv_ref,
                   act_ref, sg_ref, sv_ref, xn_ref, *, tiles_per_seq):
    tm = h_ref.shape[0]
    i = pl.program_id(0)

    @pl.when(pl.program_id(1) == 0)
    def _():
        g = g_ref[...]
        halo = _rmsn(halo_ref[...], g)
        halo = jnp.where(i % tiles_per_seq == 0, 0.0, halo)
        xn_ref[:HALO, :] = halo.astype(BF16)
        xn_ref[HALO:, :] = _rmsn(h_ref[...], g).astype(BF16)

    xn = xn_ref[...]

    def conv(u, c_ref, sl):
        out = pltpu.roll(u, 2, axis=0) * c_ref[0:1, sl]
        out = out + pltpu.roll(u, 1, axis=0) * c_ref[1:2, sl]
        out = out + u * c_ref[2:3, sl]
        return out[HALO:]

    for c in range(0, act_ref.shape[1], COL_SUB):
        sl = slice(c, c + COL_SUB)
        ug = _dot(xn, wg_ref[:, sl])
        uv = _dot(xn, wv_ref[:, sl])
        gate = conv(ug, cg_ref, sl)
        val = conv(uv, cv_ref, sl)
        act_ref[:, sl] = (gate * jax.nn.sigmoid(gate) * val).astype(act_ref.dtype)
        sg_ref[:, sl] = ug[HALO + tm - 2:, :]
        sv_ref[:, sl] = uv[HALO + tm - 2:, :]


def _ffn_up_prompt(h, W, layer, B, T):
    M, K = h.shape
    tm = _pick(T, 1024, HALO)
    tn = _pick(D_FF, 512, LANES)
    nt = T // tm
    nj = D_FF // tn
    blocks = (_nbytes((tm + HALO, K), F32) + 2 * _nbytes((K, tn), BF16) + _nbytes((tm, tn), BF16)
              + 4 * _nbytes((tm + HALO, tn), F32))
    scratch = _nbytes((tm + HALO, K), BF16)
    hb = tm // HALO
    act, sg, sv = pl.pallas_call(
        functools.partial(_ffn_up_kernel, tiles_per_seq=nt),
        grid=(M // tm, nj),
        in_specs=[pl.BlockSpec((tm, K), lambda i, j: (i, 0)),
                  pl.BlockSpec((HALO, K), lambda i, j: (jnp.maximum(i * hb - 1, 0), 0)),
                  pl.BlockSpec((None, 1, K), lambda i, j: (layer, 0, 0)),
                  pl.BlockSpec((None, K, tn), lambda i, j: (layer, 0, j)),
                  pl.BlockSpec((None, K, tn), lambda i, j: (layer, 0, nj + j)),
                  pl.BlockSpec((None, 3, tn), lambda i, j: (layer, 0, j)),
                  pl.BlockSpec((None, 3, tn), lambda i, j: (layer, 0, nj + j))],
        out_specs=[pl.BlockSpec((tm, tn), lambda i, j: (i, j)),
                   pl.BlockSpec((None, 2, tn), lambda i, j: (i, 0, j)),
                   pl.BlockSpec((None, 2, tn), lambda i, j: (i, 0, j))],
        out_shape=[jax.ShapeDtypeStruct((M, D_FF), BF16),
                   jax.ShapeDtypeStruct((M // tm, 2, D_FF), F32),
                   jax.ShapeDtypeStruct((M // tm, 2, D_FF), F32)],
        scratch_shapes=[pltpu.VMEM((tm + HALO, K), BF16)],
        compiler_params=_params(blocks, scratch, ("parallel", "arbitrary")),
        name="ffn_up",
    )(h, h, W["g_ffn"], W["w_up"], W["w_up"], W["w_ffn_conv"], W["w_ffn_conv"])
    return act, sg[nt - 1::nt], sv[nt - 1::nt]


def _ffn_conv_decode_kernel(ug_ref, uv_ref, s0g_ref, s0v_ref, s1g_ref, s1v_ref, cg_ref, cv_ref,
                            act_ref):
    gate = s0g_ref[...] * cg_ref[0:1, :] + s1g_ref[...] * cg_ref[1:2, :] + ug_ref[...] * cg_ref[2:3, :]
    val = s0v_ref[...] * cv_ref[0:1, :] + s1v_ref[...] * cv_ref[1:2, :] + uv_ref[...] * cv_ref[2:3, :]
    act_ref[...] = (gate * jax.nn.sigmoid(gate) * val).astype(act_ref.dtype)


def _ffn_conv_decode(u, state2d, w_ffn_conv, layer):
    Bd = u.shape[0]
    tn = _pick(D_FF, 1024, LANES)
    nj = D_FF // tn
    blocks = 6 * _nbytes((Bd, tn), F32) + _nbytes((Bd, tn), BF16)
    u_spec = lambda off: pl.BlockSpec((Bd, tn), lambda j: (0, off + j))
    s_spec = lambda off: pl.BlockSpec((None, Bd, tn), lambda j: (layer, 0, off + j))
    c_spec = lambda off: pl.BlockSpec((None, 3, tn), lambda j: (layer, 0, off + j))
    return pl.pallas_call(
        _ffn_conv_decode_kernel,
        grid=(nj,),
        in_specs=[u_spec(0), u_spec(nj), s_spec(0), s_spec(nj), s_spec(2 * nj), s_spec(3 * nj),
                  c_spec(0), c_spec(nj)],
        out_specs=pl.BlockSpec((Bd, tn), lambda j: (0, j)),
        out_shape=jax.ShapeDtypeStruct((Bd, D_FF), BF16),
        compiler_params=_params(blocks, 0, ("parallel",)),
        name="ffn_conv_decode",
    )(u, u, state2d, state2d, state2d, state2d, w_ffn_conv, w_ffn_conv)


def _ple_kernel(h_ref, g_ref, wg_ref, p_ref, wp_ref, o_ref, xn_ref):
    tn = o_ref.shape[1]
    j = pl.program_id(1)

    @pl.when(j == 0)
    def _():
        xn_ref[...] = _rmsn(h_ref[...], g_ref[...]).astype(BF16)

    tm = o_ref.shape[0]
    rs = min(ROW_SUB, tm)
    for r in range(0, tm, rs):
        rows = slice(r, r + rs)
        gate = jax.nn.sigmoid(_dot(xn_ref[rows, :], wg_ref[...]))
        emb = _dot(p_ref[rows, :].astype(BF16), wp_ref[...])
        res = h_ref[rows, pl.ds(pl.multiple_of(j * tn, tn), tn)]
        o_ref[rows, :] = res + gate * emb


def _ple(h, p, W, layer):
    M, K = h.shape
    tm = _pick(M, 1024, 16)
    tn = 512
    blocks = (_nbytes((tm, K), F32) + _nbytes((K, tn), BF16) + _nbytes((tm, PLE_DIM), F32)
              + _nbytes((PLE_DIM, tn), BF16) + _nbytes((tm, tn), F32))
    return pl.pallas_call(
        _ple_kernel,
        grid=(M // tm, K // tn),
        in_specs=[pl.BlockSpec((tm, K), lambda i, j: (i, 0)),
                  pl.BlockSpec((None, 1, K), lambda i, j: (layer, 0, 0)),
                  pl.BlockSpec((None, K, tn), lambda i, j: (layer, 0, j)),
                  pl.BlockSpec((None, tm, PLE_DIM), lambda i, j: (layer, i, 0)),
                  pl.BlockSpec((None, PLE_DIM, tn), lambda i, j: (layer, 0, j))],
        out_specs=pl.BlockSpec((tm, tn), lambda i, j: (i, j)),
        out_shape=jax.ShapeDtypeStruct((M, K), F32),
        scratch_shapes=[pltpu.VMEM((tm, K), BF16)],
        compiler_params=_params(blocks, _nbytes((tm, K), BF16), ("parallel", "arbitrary")),
        name="ple",
    )(h, W["g_ple"], W["w_ple_gate"], p, W["w_ple"])


def _final_norm_kernel(h_ref, g_ref, o_ref):
    o_ref[...] = _rmsn(h_ref[...], g_ref[...])


def _final_norm(h, g):
    M, K = h.shape
    tm = _pick(M, 512)
    return pl.pallas_call(
        _final_norm_kernel,
        grid=(M // tm,),
        in_specs=[pl.BlockSpec((tm, K), lambda i: (i, 0)),
                  pl.BlockSpec((1, K), lambda i: (0, 0))],
        out_specs=pl.BlockSpec((tm, K), lambda i: (i, 0)),
        out_shape=jax.ShapeDtypeStruct((M, K), F32),
        compiler_params=_params(2 * _nbytes((tm, K), F32), 0, ("parallel",)),
        name="final_norm",
    )(h, g)


def _w_in_segments():
    offs = [0]
    for s in IN_SPLITS:
        offs.append(offs[-1] + s)
    dst = {0: Z_CB, 1: Z_CC, 2: Z_CH, 10: Z_GA, 11: Z_GA + D_MODEL, 12: Z_GA + 2 * D_MODEL,
           7: Z_GV, 9: Z_GR, 3: Z_QA, 5: Z_GQ, 6: Z_GK, 4: Z_KVG, 8: Z_GG}
    return [(offs[n], IN_SPLITS[n], d) for n, d in dst.items()]


def _w_in_relayout_kernel(src_ref, valid_ref, *refs):
    o_ref = refs[-1]
    j = pl.program_id(1)
    for g, w_ref in enumerate(refs[:-1]):
        t = w_ref[0].T
        col = lax.broadcasted_iota(jnp.int32, t.shape, 1)
        keep = col < valid_ref[j * RELAYOUT_GROUP + g]
        o_ref[:, g * LANES:(g + 1) * LANES] = jnp.where(keep, t, 0.0).astype(o_ref.dtype)


def _w_in_relayout(w_in):
    depth, K, N = w_in.shape
    w_t = jnp.swapaxes(w_in, 1, 2)
    src = [0] * (Z_DIM // LANES)
    valid = [0] * (Z_DIM // LANES)
    for s, width, d in _w_in_segments():
        assert s % 8 == 0 and d % LANES == 0
        for o in range(0, width, LANES):
            src[(d + o) // LANES] = (s + o) // 8
            valid[(d + o) // LANES] = min(LANES, width - o)
    assert max(src) * 8 + LANES <= N
    G = RELAYOUT_GROUP
    assert (Z_DIM // LANES) % G == 0
    blocks = G * (_nbytes((LANES, K), F32) + _nbytes((K, LANES), BF16) + _nbytes((K, LANES), F32))

    def window(g):
        return pl.BlockSpec((pl.Element(1), pl.Element(LANES), pl.Element(K)),
                            lambda d, j, src, valid: (d, src[j * G + g] * 8, 0))

    grid_spec = pltpu.PrefetchScalarGridSpec(
        num_scalar_prefetch=2,
        grid=(depth, Z_DIM // (G * LANES)),
        in_specs=[window(g) for g in range(G)],
        out_specs=pl.BlockSpec((None, K, G * LANES), lambda d, j, src, valid: (d, 0, j)),
    )
    return pl.pallas_call(
        _w_in_relayout_kernel,
        grid_spec=grid_spec,
        out_shape=jax.ShapeDtypeStruct((depth, K, Z_DIM), BF16),
        compiler_params=_params(blocks, 0, ("parallel", "parallel")),
        name="w_in_relayout",
    )(jnp.asarray(src, jnp.int32), jnp.asarray(valid, jnp.int32), *([w_t] * G))


def _prep_weights(w):
    depth = w["w_in"].shape[0]
    H, R = MLA_HEADS, MLA_KV_RANK
    w_in = _w_in_relayout(w["w_in"])

    wq = w["w_qb"].reshape(depth, MLA_Q_RANK, H, MLA_QK)
    half = MLA_ROPE // 2
    x1 = wq[..., MLA_NOPE:MLA_NOPE + half]
    x2 = wq[..., MLA_NOPE + half:]
    pad_q = jnp.zeros((depth, MLA_Q_RANK, H, HEAD_PAD - MLA_QK), F32)
    wq_p = jnp.concatenate([wq, pad_q], axis=-1)
    wq_sw = jnp.concatenate([jnp.zeros_like(wq[..., :MLA_NOPE]), -x2, x1, pad_q], axis=-1)
    wq2 = jnp.concatenate([wq_p.reshape(depth, MLA_Q_RANK, H * HEAD_PAD),
                           wq_sw.reshape(depth, MLA_Q_RANK, H * HEAD_PAD)], axis=-1).astype(BF16)

    wkv = w["w_kvb"].reshape(depth, R, H, MLA_NOPE + MLA_V)
    w_uk, w_uv = wkv[..., :MLA_NOPE], wkv[..., MLA_NOPE:]
    wk_top = jnp.concatenate([w_uk, jnp.zeros((depth, R, H, HEAD_PAD - MLA_NOPE), F32)], axis=-1)
    eye = jnp.eye(MLA_ROPE, dtype=F32)
    copy = jnp.concatenate([jnp.zeros((MLA_ROPE, MLA_NOPE), F32), eye,
                            jnp.zeros((MLA_ROPE, HEAD_PAD - MLA_QK), F32)], axis=-1)
    copy = jnp.broadcast_to(copy[:, None, :], (MLA_ROPE, H, HEAD_PAD))
    wk_bot = jnp.concatenate([copy, jnp.zeros((LANES - MLA_ROPE, H, HEAD_PAD), F32)], axis=0)
    wk = jnp.concatenate([wk_top, jnp.broadcast_to(wk_bot[None], (depth, LANES, H, HEAD_PAD))],
                         axis=1).reshape(depth, R + LANES, H * HEAD_PAD).astype(BF16)
    wv = w_uv.reshape(depth, R, H * MLA_V).astype(BF16)

    QC = R + LANES
    uk_t = jnp.transpose(w_uk, (0, 2, 3, 1))
    top = jnp.concatenate([uk_t, jnp.zeros((depth, H, MLA_NOPE, LANES), F32)], axis=-1)
    mid = jnp.concatenate([jnp.zeros((MLA_ROPE, R), F32), eye,
                           jnp.zeros((MLA_ROPE, LANES - MLA_ROPE), F32)], axis=-1)
    mid = jnp.broadcast_to(mid[None, None], (depth, H, MLA_ROPE, QC))
    bot = jnp.zeros((depth, H, HEAD_PAD - MLA_QK, QC), F32)
    wabs = jnp.concatenate([top, mid, bot], axis=2).astype(BF16)
    uv_h = jnp.transpose(w_uv, (0, 2, 1, 3)).reshape(depth, H // 2, 2, R, MLA_V)
    zed = jnp.zeros_like(uv_h[:, :, 0])
    wuv_bd = jnp.concatenate([jnp.concatenate([uv_h[:, :, 0], zed], axis=-1),
                              jnp.concatenate([zed, uv_h[:, :, 1]], axis=-1)], axis=2).astype(BF16)

    wg2 = jnp.concatenate([w["w_gla_g2"],
                           jnp.zeros((depth, LANES - GLA_GATE_RANK, GLA_HEADS * GLA_DK), F32)],
                          axis=1).astype(BF16)
    row = lambda a: a[:, None, :]
    out = dict(w_in=w_in, wq2=wq2, wk=wk, wv=wv, wabs=wabs, wuv_bd=wuv_bd, wg2=wg2,
               g_mix=row(w["g_mix"]), g_qa=row(w["g_qa"]), g_kva=row(w["g_kva"]),
               b_gla_g=row(w["b_gla_g"]), g_gla_o=row(w["g_gla_o"]), g_ffn=row(w["g_ffn"]),
               g_ple=row(w["g_ple"]), w_conv=w["w_conv"], w_ffn_conv=w["w_ffn_conv"])
    for name in ("w_br_conv", "w_br_mla", "w_br_gla", "w_o", "w_up", "w_down", "w_ple_gate", "w_ple"):
        out[name] = w[name].astype(BF16)
    return out


def _rope_tables(pos):
    half = MLA_ROPE // 2
    inv = ROPE_THETA ** (-jnp.arange(half, dtype=F32) / half)
    ang = pos.astype(F32)[:, None] * inv[None, :]
    cos, sin = jnp.cos(ang), jnp.sin(ang)
    n = pos.shape[0]
    ones = jnp.ones((n, MLA_NOPE), F32)
    zq = jnp.zeros((n, HEAD_PAD - MLA_QK), F32)
    zk = jnp.zeros((n, LANES - MLA_ROPE), F32)
    c_q = jnp.concatenate([ones, cos, cos, zq], axis=1)
    s_q = jnp.concatenate([jnp.zeros_like(ones), sin, sin, zq], axis=1)
    c_k = jnp.concatenate([cos, cos, zk], axis=1)
    s_k = jnp.concatenate([-sin, sin, zk], axis=1)
    return c_q, s_q, c_k, s_k


def _layer_tail(h, z, y_conv, y_mla, y_gla, W, layer):
    merged = _merge(y_conv, y_mla, y_gla, z, W, layer)
    return _matmul_res(merged, W["w_o"], layer, h, tm_pref=512, tn_pref=2048, name="out_proj")


def _prompt_trunk(x, p, W, g_final):
    B, T, D = x.shape
    depth = W["w_in"].shape[0]
    M = B * T
    h = x.reshape(M, D)
    p2 = p.reshape(depth, M, PLE_DIM)
    tabs = _rope_tables(jnp.arange(T, dtype=jnp.int32))
    rows, conv_st, gla_st, ffn_st = [], [], [], []
    for i in range(depth):
        z = _rms_matmul(h, W["g_mix"], W["w_in"], i, tm_pref=1024, tn_pref=1024, name="in_proj")
        y_conv, cst = _conv_prompt(z, W["w_conv"], i, B, T)
        q, mla_rows, k, v = _mla_prep(z, W, i, tabs, T, with_kv=True)
        y_mla = _mla_attn(q, k, v, B, T)
        y_gla, s_new = _gla_prompt(z, W, i, B, T)
        h = _layer_tail(h, z, y_conv, y_mla, y_gla, W, i)
        act, sg, sv = _ffn_up_prompt(h, W, i, B, T)
        h = _matmul_res(act, W["w_down"], i, h, tm_pref=512, tn_pref=1024, name="ffn_down")
        h = _ple(h, p2, W, i)
        rows.append(mla_rows.reshape(B, T, MLA_CACHE_DIM))
        conv_st.append(cst)
        gla_st.append(s_new)
        ffn_st.append(jnp.concatenate([sg, sv], axis=-1))
    y = _final_norm(h, g_final).reshape(B, T, D)
    return y, jnp.stack(rows), jnp.stack(conv_st), jnp.stack(gla_st), jnp.stack(ffn_st)


def _decode_trunk(x, p, W, g_final, cache, page_table, st_conv, st_gla, st_ffn):
    Bd, T, D = x.shape
    assert T == 1, "decode path handles one new token per request"
    depth = W["w_in"].shape[0]
    n_pages = page_table.shape[1]
    past_len = n_pages * cache.shape[2]
    h = x.reshape(Bd, D)
    p2 = p.reshape(depth, Bd, PLE_DIM)
    tabs = _rope_tables(jnp.full((Bd,), past_len, dtype=jnp.int32))
    pt_flat = page_table.reshape(-1)
    cache_t = jnp.swapaxes(cache, 2, 3)
    st_conv2 = st_conv.reshape(depth, Bd, 2 * CONV_DIM)
    st_ffn2 = st_ffn.reshape(depth, Bd, 4 * D_FF)
    rows, conv_st, gla_st, ffn_st = [], [], [], []
    for i in range(depth):
        z = _rms_matmul(h, W["g_mix"], W["w_in"], i, tm_pref=1024, tn_pref=1024, name="in_proj_decode")
        y_conv, cst = _conv_decode(z, st_conv2, W["w_conv"], i)
        q, mla_rows = _mla_prep(z, W, i, tabs, Bd, with_kv=False)
        qcat = _absorb(q, W["wabs"], i)
        o_lat = _dec_attn(pt_flat, qcat, mla_rows, cache_t, i, n_pages)
        y_mla = _dec_out(o_lat.reshape(Bd, MLA_HEADS * MLA_KV_RANK), W["wuv_bd"], i)
        y_gla, s_new = _gla_decode(z, st_gla, W, i)
        h = _layer_tail(h, z, y_conv, y_mla, y_gla, W, i)
        u = _rms_matmul(h, W["g_ffn"], W["w_up"], i, tm_pref=1024, tn_pref=1024, name="ffn_up_decode")
        act = _ffn_conv_decode(u, st_ffn2, W["w_ffn_conv"], i)
        h = _matmul_res(act, W["w_down"], i, h, tm_pref=512, tn_pref=1024, name="ffn_down_decode")
        h = _ple(h, p2, W, i)
        rows.append(mla_rows.reshape(Bd, 1, MLA_CACHE_DIM))
        conv_st.append(cst.reshape(Bd, 2, CONV_DIM))
        gla_st.append(s_new)
        ffn_st.append(jnp.stack([st_ffn[i, :, 1, :], u], axis=1))
    y = _final_norm(h, g_final).reshape(Bd, 1, D)
    return y, jnp.stack(rows), jnp.stack(conv_st), jnp.stack(gla_st), jnp.stack(ffn_st)


def kernel(x_prompt, x_sample, cache_mla, state_conv, state_gla, state_ffn, page_table, p_prompt, p_sample, g_mix, w_in, w_conv, g_qa, w_qb, g_kva, w_kvb, w_gla_g2, b_gla_g, g_gla_o, w_br_conv, w_br_mla, w_br_gla, w_o, g_ffn, w_up, w_ffn_conv, w_down, g_ple, w_ple_gate, w_ple, g_final):
    W = _prep_weights(dict(
        g_mix=g_mix, w_in=w_in, w_conv=w_conv, g_qa=g_qa, w_qb=w_qb, g_kva=g_kva, w_kvb=w_kvb,
        w_gla_g2=w_gla_g2, b_gla_g=b_gla_g, g_gla_o=g_gla_o, w_br_conv=w_br_conv,
        w_br_mla=w_br_mla, w_br_gla=w_br_gla, w_o=w_o, g_ffn=g_ffn, w_up=w_up,
        w_ffn_conv=w_ffn_conv, w_down=w_down, g_ple=g_ple, w_ple_gate=w_ple_gate, w_ple=w_ple))
    g_fin = g_final[None, :]
    y_p, mla_p, conv_p, gla_p, ffn_p = _prompt_trunk(x_prompt, p_prompt, W, g_fin)
    y_s, mla_s, conv_s, gla_s, ffn_s = _decode_trunk(
        x_sample, p_sample, W, g_fin, cache_mla, page_table, state_conv, state_gla, state_ffn)
    return (y_p, y_s, mla_p, mla_s, conv_p, conv_s, gla_p, gla_s, ffn_p, ffn_s)
```
